```python
import math
import jax, jax.numpy as jnp
from jax import lax
import numpy as np

D_MODEL = 1024
BATCH = 16
SEQ = 4096
DEPTH = 1

GLA_HEADS = 4
GLA_DK = 128
GLA_DV = 256
GLA_RANK = 16
GLA_GATE_NORM = 16.0
GLA_CHUNK = 64
DIFF_HEADS = 8
DIFF_DH = 64
DIFF_DV = 2 * DIFF_DH
ROT_DIM = DIFF_DH // 4
ROPE_THETA = 500000.0
Q_BLOCK = 128
D_FF = 4 * D_MODEL
EPS = 1e-6

GLA_QK = GLA_HEADS * GLA_DK
GLA_V = GLA_HEADS * GLA_DV
DIFF_QK = DIFF_HEADS * 2 * DIFF_DH
DIFF_V = DIFF_HEADS * DIFF_DV
IN_SIZES = (GLA_QK, GLA_QK, GLA_V, GLA_V, GLA_RANK, DIFF_QK, DIFF_QK, DIFF_V, 2 * D_MODEL)
D_IN = sum(IN_SIZES)
SPLIT_POINTS = [int(s) for s in np.cumsum(IN_SIZES)[:-1]]

kernel_name = "hybrid_gla_diffattn_gated_block"


def rmsnorm(x, w):
    xf = x.astype(jnp.float32)
    y = xf * lax.rsqrt(jnp.mean(xf * xf, axis=-1, keepdims=True) + EPS)
    return (y * w.astype(jnp.float32)).astype(x.dtype)


def rope_tables(positions):
    inv_freq = ROPE_THETA ** (-jnp.arange(0, ROT_DIM, 2, dtype=jnp.float32) / ROT_DIM)
    ang = positions.astype(jnp.float32)[..., None] * inv_freq
    return jnp.cos(ang)[:, :, None, None, :], jnp.sin(ang)[:, :, None, None, :]


def apply_partial_rope(t, cos, sin):
    tr, tp = t[..., :ROT_DIM], t[..., ROT_DIM:]
    t1, t2 = tr[..., :ROT_DIM // 2], tr[..., ROT_DIM // 2:]
    rot = jnp.concatenate([t1 * cos - t2 * sin, t2 * cos + t1 * sin], axis=-1).astype(t.dtype)
    return jnp.concatenate([rot, tp], axis=-1)


def gla_chunked(q, k, v, log_a):
    B, S, H, _ = q.shape
    C = GLA_CHUNK
    N = S // C

    def to_chunks(t):
        return t.astype(jnp.float32).reshape(B, N, C, H, -1).transpose(0, 3, 1, 2, 4)

    qc = to_chunks(q) * (GLA_DK ** -0.5)
    kc, vc, gc = to_chunks(k), to_chunks(v), to_chunks(log_a)
    G = jnp.cumsum(gc, axis=3)
    G_mid = G[:, :, :, C // 2:C // 2 + 1, :]
    G_last = G[:, :, :, C - 1:C, :]
    qi = qc * jnp.exp(G - G_mid)
    kj = kc * jnp.exp(G_mid - G)
    A = jnp.einsum('bhnid,bhnjd->bhnij', qi, kj)
    causal = jnp.tril(jnp.ones((C, C), dtype=bool))
    A = jnp.where(causal, A, 0.0)
    o_intra = jnp.einsum('bhnij,bhnjv->bhniv', A, vc)
    q_in = qc * jnp.exp(G)
    k_st = kc * jnp.exp(G_last - G)
    decay = jnp.exp(G_last[:, :, :, 0, :])

    def step(state, xs):
        qn, kn, vn, dn = xs
        o = jnp.einsum('bhcd,bhdv->bhcv', qn, state)
        state = dn[..., None] * state + jnp.einsum('bhcd,bhcv->bhdv', kn, vn)
        return state, o

    xs = (jnp.moveaxis(q_in, 2, 0), jnp.moveaxis(k_st, 2, 0),
          jnp.moveaxis(vc, 2, 0), jnp.moveaxis(decay, 2, 0))
    state0 = jnp.zeros((B, H, GLA_DK, GLA_DV), jnp.float32)
    _, o_inter = lax.scan(step, state0, xs)
    o = o_intra + jnp.moveaxis(o_inter, 0, 2)
    return o.transpose(0, 2, 3, 1, 4).reshape(B, S, H, GLA_DV)


def diff_attention(q, k, v, lam):
    B, S, H, _, dh = q.shape
    NB = S // Q_BLOCK
    scale = dh ** -0.5
    qb = q.reshape(B, NB, Q_BLOCK, H, 2, dh).transpose(1, 0, 3, 4, 2, 5)
    kf = k.transpose(0, 2, 3, 1, 4)
    vf = v.transpose(0, 2, 1, 3)
    kpos = jnp.arange(S)

    def block(args):
        qblk, n = args
        s = jnp.einsum('bhcqd,bhckd->bhcqk', qblk, kf).astype(jnp.float32) * scale
        qpos = n * Q_BLOCK + jnp.arange(Q_BLOCK)
        mask = kpos[None, :] <= qpos[:, None]
        s = jnp.where(mask, s, jnp.finfo(jnp.float32).min)
        p = jax.nn.softmax(s, axis=-1)
        attn = p[:, :, 0] - lam * p[:, :, 1]
        return jnp.einsum('bhqk,bhkv->bhqv', attn.astype(vf.dtype), vf)

    outs = lax.map(block, (qb, jnp.arange(NB)))
    return outs.transpose(1, 0, 3, 2, 4).reshape(B, S, H, DIFF_DV)


def setup_inputs(seed: int = 0) -> dict:
    key = jax.random.key(seed)
    ks = jax.random.split(key, 24)
    f32 = jnp.float32

    def nrm(k, shape, scale):
        return jax.random.normal(k, shape, f32) * scale

    def gain(k, n):
        return 1.0 + 0.02 * jax.random.normal(k, (DEPTH, n), f32)

    x = jax.random.normal(ks[0], (BATCH, SEQ, D_MODEL), f32)
    offsets = jax.random.randint(ks[1], (BATCH, 1), 0, 1024, dtype=jnp.int32)
    positions = offsets + jnp.arange(SEQ, dtype=jnp.int32)[None, :]
    return {
        "x": x,
        "positions": positions,
        "w_in": nrm(ks[2], (DEPTH, D_MODEL, D_IN), D_MODEL ** -0.5),
        "b_gate": nrm(ks[3], (DEPTH, 2 * D_MODEL), 0.01),
        "w_gk_up": nrm(ks[4], (DEPTH, GLA_RANK, GLA_QK), GLA_RANK ** -0.5),
        "b_gk": nrm(ks[5], (DEPTH, GLA_QK), 0.01),
        "gla_norm_w": gain(ks[6], GLA_DV),
        "lambda_q1": nrm(ks[7], (DEPTH, DIFF_DH), 0.1),
        "lambda_k1": nrm(ks[8], (DEPTH, DIFF_DH), 0.1),
        "lambda_q2": nrm(ks[9], (DEPTH, DIFF_DH), 0.1),
        "lambda_k2": nrm(ks[10], (DEPTH, DIFF_DH), 0.1),
        "diff_norm_w": gain(ks[11], DIFF_DV),
        "w_branch_a": nrm(ks[12], (DEPTH, GLA_V, D_MODEL), GLA_V ** -0.5),
        "w_branch_b": nrm(ks[13], (DEPTH, DIFF_V, D_MODEL), DIFF_V ** -0.5),
        "w_out": nrm(ks[14], (DEPTH, D_MODEL, D_MODEL), D_MODEL ** -0.5),
        "pre_mix_w": gain(ks[15], D_MODEL),
        "post_mix_w": gain(ks[16], D_MODEL),
        "pre_mlp_w": gain(ks[17], D_MODEL),
        "post_mlp_w": gain(ks[18], D_MODEL),
        "w_up": nrm(ks[19], (DEPTH, D_MODEL, D_FF), D_MODEL ** -0.5),
        "w_down": nrm(ks[20], (DEPTH, D_FF, D_MODEL), D_FF ** -0.5),
    }


def reference(x, positions, w_in, b_gate, w_gk_up, b_gk, gla_norm_w,
              lambda_q1, lambda_k1, lambda_q2, lambda_k2, diff_norm_w,
              w_branch_a, w_branch_b, w_out,
              pre_mix_w, post_mix_w, pre_mlp_w, post_mlp_w, w_up, w_down):
    B, S, _ = x.shape
    cos, sin = rope_tables(positions)
    for l in range(DEPTH):
        lam_init = 0.8 - 0.6 * math.exp(-0.3 * l)
        u = rmsnorm(x, pre_mix_w[l])
        proj = u @ w_in[l]
        (g_q, g_k, g_v, g_og, g_low, d_q, d_k, d_v, gates) = jnp.split(proj, SPLIT_POINTS, axis=-1)

        log_a = jax.nn.log_sigmoid((g_low @ w_gk_up[l] + b_gk[l]).astype(jnp.float32)) / GLA_GATE_NORM
        o_a = gla_chunked(g_q.reshape(B, S, GLA_HEADS, GLA_DK),
                          g_k.reshape(B, S, GLA_HEADS, GLA_DK),
                          g_v.reshape(B, S, GLA_HEADS, GLA_DV),
                          log_a.reshape(B, S, GLA_HEADS, GLA_DK)).astype(x.dtype)
        o_a = rmsnorm(o_a, gla_norm_w[l]) * jax.nn.silu(g_og.reshape(B, S, GLA_HEADS, GLA_DV))
        y_a = o_a.reshape(B, S, GLA_V) @ w_branch_a[l]

        dq = apply_partial_rope(d_q.reshape(B, S, DIFF_HEADS, 2, DIFF_DH), cos, sin)
        dk = apply_partial_rope(d_k.reshape(B, S, DIFF_HEADS, 2, DIFF_DH), cos, sin)
        lam = (jnp.exp(jnp.sum(lambda_q1[l].astype(jnp.float32) * lambda_k1[l].astype(jnp.float32)))
               - jnp.exp(jnp.sum(lambda_q2[l].astype(jnp.float32) * lambda_k2[l].astype(jnp.float32)))
               + lam_init)
        o_b = diff_attention(dq, dk, d_v.reshape(B, S, DIFF_HEADS, DIFF_DV), lam)
        o_b = rmsnorm(o_b, diff_norm_w[l]) * (1.0 - lam_init)
        y_b = o_b.reshape(B, S, DIFF_V) @ w_branch_b[l]

        gate_a, gate_b = jnp.split(jax.nn.sigmoid(gates + b_gate[l]), 2, axis=-1)
        mixed = (gate_a * y_a + gate_b * y_b) @ w_out[l]
        x = x + rmsnorm(mixed, post_mix_w[l])

        h = rmsnorm(x, pre_mlp_w[l])
        f = jnp.square(jax.nn.relu(h @ w_up[l])) @ w_down[l]
        x = x + rmsnorm(f, post_mlp_w[l])
    return x
```

```python
import functools
import math

import jax
import jax.numpy as jnp
from jax import lax
from jax.experimental import pallas as pl
from jax.experimental.pallas import tpu as pltpu

F32 = jnp.float32
BF16 = jnp.bfloat16

D_MODEL = 1024
GLA_HEADS = 4
GLA_DK = 128
GLA_DV = 256
GLA_RANK = 16
GLA_GATE_NORM = 16.0
GLA_CHUNK = 64
DIFF_HEADS = 8
DIFF_DH = 64
DIFF_DV = 2 * DIFF_DH
ROT_DIM = DIFF_DH // 4
ROT_HALF = ROT_DIM // 2
ROPE_THETA = 500000.0
D_FF = 4 * D_MODEL
EPS = 1e-6

GLA_QK = GLA_HEADS * GLA_DK
GLA_V = GLA_HEADS * GLA_DV
DIFF_QK = DIFF_HEADS * 2 * DIFF_DH
DIFF_V = DIFF_HEADS * DIFF_DV
IN_SIZES = (GLA_QK, GLA_QK, GLA_V, GLA_V, GLA_RANK, DIFF_QK, DIFF_QK, DIFF_V, 2 * D_MODEL)

LANES = 128
TOK_BLK = 256
GLA_SUPER = 2 * GLA_CHUNK
MERGE_TM = 512
MLP_TM = 512
FF_CHUNK = 1024
VMEM_LIMIT = 56 * 1024 * 1024
NEG_BIG = -1e30


def _rms(xf, w):
    ms = jnp.mean(xf * xf, axis=-1, keepdims=True)
    return xf * lax.rsqrt(ms + EPS) * w


def _dot(a, b):
    return jnp.dot(a, b, preferred_element_type=F32)


def _dot_nt(a, b):
    return lax.dot_general(a, b, (((1,), (1,)), ((), ())), preferred_element_type=F32)


def _const_spec(shape):
    nd = len(shape)
    return pl.BlockSpec(shape, lambda *_: (0,) * nd, pipeline_mode=pl.Buffered(1))


_N_GQ, _N_GK, _N_GV, _N_OG, _N_DK, _N_GATE = 0, 512, 1024, 2048, 3072, 4096
_N_END = 6144
_T_GV, _T_DQ, _T_DV, _T_END = 0, 1024, 2048, 3072


def _inproj_body(x_ref, pmw_ref, wnat_ref, wlow_ref, wgk_ref, bgk_ref, bgate_ref, wt_ref,
                 cn_ref, s1_ref, s2_ref, ct_ref, st_ref,
                 gq_ref, gk_ref, gv_ref, og_ref, la_ref, dk_ref, gates_ref,
                 gvt_ref, dqt_ref, dvt_ref):
    u = _rms(x_ref[...], pmw_ref[...]).astype(BF16)

    def mm(lo, hi):
        return _dot(u, wnat_ref[:, lo:hi])

    gq_ref[...] = (mm(_N_GQ, _N_GK) * (GLA_DK ** -0.5)).astype(BF16)
    gk_ref[...] = mm(_N_GK, _N_GV).astype(BF16)
    gv_ref[...] = mm(_N_GV, _N_OG).astype(BF16)
    og = mm(_N_OG, _N_DK)
    og_ref[...] = (og * jax.nn.sigmoid(og)).astype(BF16)

    g_low = _dot(u, wlow_ref[...]).astype(BF16)
    z = _dot(g_low, wgk_ref[...]) + bgk_ref[...]
    la_ref[...] = (jnp.minimum(z, 0.0) - jnp.log(1.0 + jnp.exp(-jnp.abs(z)))) * (1.0 / GLA_GATE_NORM)

    dk = mm(_N_DK, _N_GATE)
    c, s1, s2 = cn_ref[...], s1_ref[...], s2_ref[...]
    for h in range(DIFF_HEADS):
        t = dk[:, h * LANES:(h + 1) * LANES]
        r = t * c + pltpu.roll(t, LANES - ROT_HALF, 1) * s1 + pltpu.roll(t, ROT_HALF, 1) * s2
        dk_ref[:, h * LANES:(h + 1) * LANES] = r.astype(BF16)

    gates_ref[...] = jax.nn.sigmoid(mm(_N_GATE, _N_END) + bgate_ref[...]).astype(BF16)

    def mmt(lo, hi):
        return _dot_nt(wt_ref[lo:hi, :], u)

    gvt_ref[0] = mmt(_T_GV, _T_DQ).astype(BF16)
    dvt_ref[0] = mmt(_T_DV, _T_END).astype(BF16)
    dq = mmt(_T_DQ, _T_DV) * (DIFF_DH ** -0.5)
    ct, st = ct_ref[0], st_ref[0]
    pieces = []
    for g in range(DIFF_HEADS * 2):
        b = g * DIFF_DH
        t1, t2 = dq[b:b + ROT_HALF], dq[b + ROT_HALF:b + ROT_DIM]
        pieces += [t1 * ct - t2 * st, t2 * ct + t1 * st, dq[b + ROT_DIM:b + DIFF_DH]]
    dqt_ref[0] = jnp.concatenate(pieces, axis=0).astype(BF16)


def _inproj(x2, pmw, wnat, wlow, wgk, bgk, bgate, wt, cn, s1, s2, ct, st):
    T = x2.shape[0]
    tm = TOK_BLK
    nblk = T // tm
    row = lambda i: (i, 0)
    blk3 = lambda i: (i, 0, 0)
    in_specs = [
        pl.BlockSpec((tm, D_MODEL), row),
        _const_spec((1, D_MODEL)),
        _const_spec((D_MODEL, _N_END)),
        _const_spec((D_MODEL, LANES)),
        _const_spec((LANES, GLA_QK)),
        _const_spec((1, GLA_QK)),
        _const_spec((1, 2 * D_MODEL)),
        _const_spec((_T_END, D_MODEL)),
        pl.BlockSpec((tm, LANES), row),
        pl.BlockSpec((tm, LANES), row),
        pl.BlockSpec((tm, LANES), row),
        pl.BlockSpec((1, ROT_HALF, tm), blk3),
        pl.BlockSpec((1, ROT_HALF, tm), blk3),
    ]
    out_shape = [
        jax.ShapeDtypeStruct((T, GLA_QK), BF16),
        jax.ShapeDtypeStruct((T, GLA_QK), BF16),
        jax.ShapeDtypeStruct((T, GLA_V), BF16),
        jax.ShapeDtypeStruct((T, GLA_V), BF16),
        jax.ShapeDtypeStruct((T, GLA_QK), F32),
        jax.ShapeDtypeStruct((T, DIFF_QK), BF16),
        jax.ShapeDtypeStruct((T, 2 * D_MODEL), BF16),
        jax.ShapeDtypeStruct((nblk, GLA_V, tm), BF16),
        jax.ShapeDtypeStruct((nblk, DIFF_QK, tm), BF16),
        jax.ShapeDtypeStruct((nblk, DIFF_V, tm), BF16),
    ]
    out_specs = [
        pl.BlockSpec((tm, GLA_QK), row),
        pl.BlockSpec((tm, GLA_QK), row),
        pl.BlockSpec((tm, GLA_V), row),
        pl.BlockSpec((tm, GLA_V), row),
        pl.BlockSpec((tm, GLA_QK), row),
        pl.BlockSpec((tm, DIFF_QK), row),
        pl.BlockSpec((tm, 2 * D_MODEL), row),
        pl.BlockSpec((1, GLA_V, tm), blk3),
        pl.BlockSpec((1, DIFF_QK, tm), blk3),
        pl.BlockSpec((1, DIFF_V, tm), blk3),
    ]
    return pl.pallas_call(
        _inproj_body,
        grid=(nblk,),
        in_specs=in_specs,
        out_specs=out_specs,
        out_shape=out_shape,
        compiler_params=pltpu.CompilerParams(
            dimension_semantics=("parallel",), vmem_limit_bytes=VMEM_LIMIT),
        name="inproj",
    )(x2, pmw, wnat, wlow, wgk, bgk, bgate, wt, cn, s1, s2, ct, st)


def _gla_body(q_ref, k_ref, v_ref, vt_ref, la_ref, og_ref, nw_ref, o_ref, state_ref):
    @pl.when(pl.program_id(1) == 0)
    def _():
        state_ref[...] = jnp.zeros_like(state_ref)

    n = GLA_SUPER
    ri = lax.broadcasted_iota(jnp.int32, (n, n), 0)
    ci = lax.broadcasted_iota(jnp.int32, (n, n), 1)
    tri = (ri >= ci).astype(BF16)
    same_chunk_causal = ((ri >= GLA_CHUNK) == (ci >= GLA_CHUNK)) & (ci <= ri)
    row = lax.broadcasted_iota(jnp.int32, (n, GLA_DK), 0)
    second = row >= GLA_CHUNK
    nw = nw_ref[...]

    for sc in range(TOK_BLK // n):
        r0 = sc * n
        la = la_ref[r0:r0 + n, :]
        hi = la.astype(BF16)
        rem = la - hi.astype(F32)
        mid = rem.astype(BF16)
        lo = (rem - mid.astype(F32)).astype(BF16)
        g_all = _dot(tri, hi) + _dot(tri, mid) + _dot(tri, lo)

        for h in range(GLA_HEADS):
            ks = slice(h * GLA_DK, (h + 1) * GLA_DK)
            vs = slice(h * GLA_DV, (h + 1) * GLA_DV)
            g = g_all[:, ks]
            q = q_ref[r0:r0 + n, ks].astype(F32)
            k = k_ref[r0:r0 + n, ks].astype(F32)
            v = v_ref[r0:r0 + n, vs]
            vt = vt_ref[0, vs, r0:r0 + n]
            g_mid = jnp.where(second, g[GLA_CHUNK + GLA_CHUNK // 2:GLA_CHUNK + GLA_CHUNK // 2 + 1],
                              g[GLA_CHUNK // 2:GLA_CHUNK // 2 + 1])
            g_c0 = g[GLA_CHUNK - 1:GLA_CHUNK]
            g_last = g[n - 1:n]

            qi = (q * jnp.exp(g - g_mid)).astype(BF16)
            kj = (k * jnp.exp(g_mid - g)).astype(BF16)
            a_diag = _dot_nt(qi, kj)
            qo = jnp.where(second, q * jnp.exp(jnp.minimum(g - g_c0, 0.0)), 0.0).astype(BF16)
            ko = jnp.where(second, 0.0, k * jnp.exp(jnp.minimum(g_c0 - g, 0.0))).astype(BF16)
            a = jnp.where(same_chunk_causal, a_diag, 0.0) + _dot_nt(qo, ko)
            o = _dot(a.astype(BF16), v)

            st = state_ref[h]
            o = o + _dot_nt((q * jnp.exp(g)).astype(BF16), st.astype(BF16))
            k_st = (k * jnp.exp(g_last - g)).astype(BF16)
            state_ref[h] = st * jnp.exp(g_last) + _dot(vt, k_st)

            og = og_ref[r0:r0 + n, vs].astype(F32)
            o_ref[r0:r0 + n, vs] = (_rms(o, nw) * og).astype(BF16)


def _gla(gq, gk, gv, gvt, la, og, nw, B, S):
    T = B * S
    tb = TOK_BLK
    nb = S // tb
    row = lambda b, i: (b * nb + i, 0)
    return pl.pallas_call(
        _gla_body,
        grid=(B, nb),
        in_specs=[
            pl.BlockSpec((tb, GLA_QK), row),
            pl.BlockSpec((tb, GLA_QK), row),
            pl.BlockSpec((tb, GLA_V), row),
            pl.BlockSpec((1, GLA_V, tb), lambda b, i: (b * nb + i, 0, 0)),
            pl.BlockSpec((tb, GLA_QK), row),
            pl.BlockSpec((tb, GLA_V), row),
            pl.BlockSpec((1, GLA_DV), lambda b, i: (0, 0)),
        ],
        out_specs=pl.BlockSpec((tb, GLA_V), row),
        out_shape=jax.ShapeDtypeStruct((T, GLA_V), BF16),
        scratch_shapes=[pltpu.VMEM((GLA_HEADS, GLA_DV, GLA_DK), F32)],
        compiler_params=pltpu.CompilerParams(
            dimension_semantics=("parallel", "arbitrary"), vmem_limit_bytes=VMEM_LIMIT),
        name="gla",
    )(gq, gk, gv, gvt, la, og, nw)


def _attn_body(qt_ref, k_ref, vt_ref, lq1_ref, lk1_ref, lq2_ref, lk2_ref, nw_ref, o_ref,
               acc1_ref, acc2_ref, *, lam_init):
    i = pl.program_id(2)
    tq = TOK_BLK
    qt = qt_ref[...]
    frow = lax.broadcasted_iota(jnp.int32, qt.shape, 0)
    q1 = jnp.where(frow < DIFF_DH, qt, jnp.zeros_like(qt))
    q2 = jnp.where(frow >= DIFF_DH, qt, jnp.zeros_like(qt))
    acc1_ref[...] = jnp.zeros_like(acc1_ref)
    acc2_ref[...] = jnp.zeros_like(acc2_ref)

    def update(s, m, l, acc_ref, vb):
        m_new = jnp.maximum(m, jnp.max(s, axis=0, keepdims=True))
        p = jnp.exp(s - m_new)
        alpha = jnp.exp(m - m_new)
        l_new = alpha * l + jnp.sum(p, axis=0, keepdims=True)
        acc_ref[...] = acc_ref[...] * alpha + _dot(vb, p.astype(BF16))
        return m_new, l_new

    def block(j, carry, masked):
        m1, l1, m2, l2 = carry
        kb = k_ref[pl.ds(pl.multiple_of(j * tq, tq), tq), :]
        vb = vt_ref[j]
        s1 = _dot(kb, q1)
        s2 = _dot(kb, q2)
        if masked:
            kr = lax.broadcasted_iota(jnp.int32, s1.shape, 0)
            qc = lax.broadcasted_iota(jnp.int32, s1.shape, 1)
            ok = kr <= qc
            s1 = jnp.where(ok, s1, NEG_BIG)
            s2 = jnp.where(ok, s2, NEG_BIG)
        m1, l1 = update(s1, m1, l1, acc1_ref, vb)
        m2, l2 = update(s2, m2, l2, acc2_ref, vb)
        return m1, l1, m2, l2

    neg = jnp.full((1, tq), NEG_BIG, F32)
    zero = jnp.zeros((1, tq), F32)
    carry = lax.fori_loop(0, i, lambda j, c: block(j, c, False), (neg, zero, neg, zero))
    m1, l1, m2, l2 = block(i, carry, True)

    lam = (jnp.exp(jnp.sum(lq1_ref[...] * lk1_ref[...], axis=1, keepdims=True))
           - jnp.exp(jnp.sum(lq2_ref[...] * lk2_ref[...], axis=1, keepdims=True)) + lam_init)
    o = acc1_ref[...] / l1 - lam * (acc2_ref[...] / l2)
    ms = jnp.mean(o * o, axis=0, keepdims=True)
    o = o * lax.rsqrt(ms + EPS) * nw_ref[...] * (1.0 - lam_init)
    o_ref[...] = o.T.astype(BF16)


def _attn(dqt, dk, dvt, lq1, lk1, lq2, lk2, nw_col, B, S, lam_init):
    T = B * S
    tq = TOK_BLK
    nq = S // tq
    H = DIFF_HEADS
    qt5 = dqt.reshape(B, nq, H, 2 * DIFF_DH, tq)
    vt5 = dvt.reshape(B, nq, H, DIFF_DV, tq)
    k3 = dk.reshape(B, S, DIFF_QK)
    vec = pl.BlockSpec((1, DIFF_DH), lambda b, h, i: (0, 0))
    out = pl.pallas_call(
        functools.partial(_attn_body, lam_init=lam_init),
        grid=(B, H, nq),
        in_specs=[
            pl.BlockSpec((None, None, None, 2 * DIFF_DH, tq), lambda b, h, i: (b, i, h, 0, 0)),
            pl.BlockSpec((None, S, 2 * DIFF_DH), lambda b, h, i: (b, 0, h)),
            pl.BlockSpec((None, nq, None, DIFF_DV, tq), lambda b, h, i: (b, 0, h, 0, 0)),
            vec, vec, vec, vec,
            pl.BlockSpec((DIFF_DV, 1), lambda b, h, i: (0, 0)),
        ],
        out_specs=pl.BlockSpec((None, tq, DIFF_DV), lambda b, h, i: (b, i, h)),
        out_shape=jax.ShapeDtypeStruct((B, S, DIFF_V), BF16),
        scratch_shapes=[pltpu.VMEM((DIFF_DV, tq), F32), pltpu.VMEM((DIFF_DV, tq), F32)],
        compiler_params=pltpu.CompilerParams(
            dimension_semantics=("parallel", "parallel", "arbitrary"), vmem_limit_bytes=VMEM_LIMIT),
        name="attn",
    )(qt5, k3, vt5, lq1, lk1, lq2, lk2, nw_col)
    return out.reshape(T, DIFF_V)


def _merge_body(x_ref, oa_ref, ob_ref, ga_ref, gb_ref, wa_ref, wb_ref, wo_ref, pw_ref, o_ref):
    ya = _dot(oa_ref[...], wa_ref[...])
    yb = _dot(ob_ref[...], wb_ref[...])
    mix = ga_ref[...].astype(F32) * ya + gb_ref[...].astype(F32) * yb
    mixed = _dot(mix.astype(BF16), wo_ref[...])
    o_ref[...] = x_ref[...] + _rms(mixed, pw_ref[...])


def _merge(x2, oa, ob, gates, wa, wb, wo, pw):
    T = x2.shape[0]
    tm = MERGE_TM
    row = lambda i: (i, 0)
    return pl.pallas_call(
        _merge_body,
        grid=(T // tm,),
        in_specs=[
            pl.BlockSpec((tm, D_MODEL), row),
            pl.BlockSpec((tm, GLA_V), row),
            pl.BlockSpec((tm, DIFF_V), row),
            pl.BlockSpec((tm, D_MODEL), lambda i: (i, 0)),
            pl.BlockSpec((tm, D_MODEL), lambda i: (i, 1)),
            _const_spec((GLA_V, D_MODEL)),
            _const_spec((DIFF_V, D_MODEL)),
            _const_spec((D_MODEL, D_MODEL)),
            _const_spec((1, D_MODEL)),
        ],
        out_specs=pl.BlockSpec((tm, D_MODEL), row),
        out_shape=jax.ShapeDtypeStruct((T, D_MODEL), F32),
        compiler_params=pltpu.CompilerParams(
            dimension_semantics=("parallel",), vmem_limit_bytes=VMEM_LIMIT),
        name="merge",
    )(x2, oa, ob, gates, gates, wa, wb, wo, pw)


def _mlp_body(x_ref, wu_ref, wd_ref, prew_ref, postw_ref, o_ref):
    x = x_ref[...]
    h = _rms(x, prew_ref[...]).astype(BF16)
    f = None
    for c in range(D_FF // FF_CHUNK):
        a = jnp.maximum(_dot(h, wu_ref[:, c * FF_CHUNK:(c + 1) * FF_CHUNK]), 0.0)
        part = _dot((a * a).astype(BF16), wd_ref[c * FF_CHUNK:(c + 1) * FF_CHUNK, :])
        f = part if f is None else f + part
    o_ref[...] = x + _rms(f, postw_ref[...])


def _mlp(x1, wu, wd, prew, postw):
    T = x1.shape[0]
    tm = MLP_TM
    row = lambda i: (i, 0)
    return pl.pallas_call(
        _mlp_body,
        grid=(T // tm,),
        in_specs=[
            pl.BlockSpec((tm, D_MODEL), row),
            _const_spec((D_MODEL, D_FF)),
            _const_spec((D_FF, D_MODEL)),
            _const_spec((1, D_MODEL)),
            _const_spec((1, D_MODEL)),
        ],
        out_specs=pl.BlockSpec((tm, D_MODEL), row),
        out_shape=jax.ShapeDtypeStruct((T, D_MODEL), F32),
        compiler_params=pltpu.CompilerParams(
            dimension_semantics=("parallel",), vmem_limit_bytes=VMEM_LIMIT),
        name="mlp",
    )(x1, wu, wd, prew, postw)


def _rope_tables(positions):
    T = positions.size
    inv_freq = ROPE_THETA ** (-jnp.arange(0, ROT_DIM, 2, dtype=F32) / ROT_DIM)
    ang = positions.astype(F32).reshape(T, 1) * inv_freq
    cos, sin = jnp.cos(ang), jnp.sin(ang)
    pad = DIFF_DH - ROT_DIM
    one = jnp.ones((T, pad), F32)
    zero = jnp.zeros((T, pad), F32)
    z8 = jnp.zeros((T, ROT_HALF), F32)
    cn = jnp.tile(jnp.concatenate([cos, cos, one], axis=1), (1, 2))
    s1 = jnp.tile(jnp.concatenate([-sin, z8, zero], axis=1), (1, 2))
    s2 = jnp.tile(jnp.concatenate([z8, sin, zero], axis=1), (1, 2))
    ct = cos.reshape(T // TOK_BLK, TOK_BLK, ROT_HALF).transpose(0, 2, 1)
    st = sin.reshape(T // TOK_BLK, TOK_BLK, ROT_HALF).transpose(0, 2, 1)
    return cn, s1, s2, ct, st


def kernel(x, positions, w_in, b_gate, w_gk_up, b_gk, gla_norm_w, lambda_q1, lambda_k1, lambda_q2,
           lambda_k2, diff_norm_w, w_branch_a, w_branch_b, w_out, pre_mix_w, post_mix_w, pre_mlp_w,
           post_mlp_w, w_up, w_down):
    B, S, D = x.shape
    T = B * S
    depth = w_in.shape[0]
    cn, s1, s2, ct, st = _rope_tables(positions)
    offs = [0]
    for n in IN_SIZES:
        offs.append(offs[-1] + n)
    (o_gq, o_gk, o_gv, o_og, o_low, o_dq, o_dk, o_dv, o_gates, o_end) = offs

    x2 = x.reshape(T, D)
    for l in range(depth):
        lam_init = 0.8 - 0.6 * math.exp(-0.3 * l)
        w = w_in[l]
        wnat = jnp.concatenate(
            [w[:, o_gq:o_gk], w[:, o_gk:o_gv], w[:, o_gv:o_og], w[:, o_og:o_low],
             w[:, o_dk:o_dv], w[:, o_gates:o_end]], axis=1).astype(BF16)
        wlow = jnp.pad(w[:, o_low:o_dq], ((0, 0), (0, LANES - GLA_RANK))).astype(BF16)
        wgk = jnp.pad(w_gk_up[l], ((0, LANES - GLA_RANK), (0, 0))).astype(BF16)
        wt = jnp.concatenate([w[:, o_gv:o_og], w[:, o_dq:o_dk], w[:, o_dv:o_gates]], axis=1).T.astype(BF16)

        gq, gk, gv, og, la, dk, gates, gvt, dqt, dvt = _inproj(
            x2, pre_mix_w[l].reshape(1, D), wnat, wlow, wgk, b_gk[l].reshape(1, -1),
            b_gate[l].reshape(1, -1), wt, cn, s1, s2, ct, st)

        o_a = _gla(gq, gk, gv, gvt, la, og, gla_norm_w[l].reshape(1, -1), B, S)
        o_b = _attn(dqt, dk, dvt,
                    lambda_q1[l].reshape(1, -1), lambda_k1[l].reshape(1, -1),
                    lambda_q2[l].reshape(1, -1), lambda_k2[l].reshape(1, -1),
                    diff_norm_w[l].reshape(-1, 1), B, S, lam_init)

        x1 = _merge(x2, o_a, o_b, gates, w_branch_a[l].astype(BF16), w_branch_b[l].astype(BF16),
                    w_out[l].astype(BF16), post_mix_w[l].reshape(1, D))
        x2 = _mlp(x1, w_up[l].astype(BF16), w_down[l].astype(BF16),
                  pre_mlp_w[l].reshape(1, D), post_mlp_w[l].reshape(1, D))
    return x2.reshape(B, S, D)
```

```python
import functools
import math

import jax
import jax.numpy as jnp
from jax import lax
from jax.experimental import pallas as pl
from jax.experimental.pallas import tpu as pltpu

F32 = jnp.float32
BF16 = jnp.bfloat16

D_MODEL = 1024
GLA_HEADS = 4
GLA_DK = 128
GLA_DV = 256
GLA_RANK = 16
GLA_GATE_NORM = 16.0
GLA_CHUNK = 64
DIFF_HEADS = 8
DIFF_DH = 64
DIFF_DV = 2 * DIFF_DH
ROT_DIM = DIFF_DH // 4
ROT_HALF = ROT_DIM // 2
ROPE_THETA = 500000.0
D_FF = 4 * D_MODEL
EPS = 1e-6

GLA_QK = GLA_HEADS * GLA_DK
GLA_V = GLA_HEADS * GLA_DV
DIFF_QK = DIFF_HEADS * 2 * DIFF_DH
DIFF_V = DIFF_HEADS * DIFF_DV
IN_SIZES = (GLA_QK, GLA_QK, GLA_V, GLA_V, GLA_RANK, DIFF_QK, DIFF_QK, DIFF_V, 2 * D_MODEL)

LANES = 128
TOK_BLK = 256
GLA_SUPER = 2 * GLA_CHUNK
ATT_HPS = 4
ATT_ONES_ROWS = 16
LOG2E = math.log2(math.e)
MERGE_TM = 512
MLP_TM = 512
FF_CHUNK = 1024
VMEM_LIMIT = 56 * 1024 * 1024
NEG_BIG = -1e30


def _rms(xf, w):
    ms = jnp.mean(xf * xf, axis=-1, keepdims=True)
    return xf * lax.rsqrt(ms + EPS) * w


def _dot(a, b):
    return jnp.dot(a, b, preferred_element_type=F32)


def _dot_nt(a, b):
    return lax.dot_general(a, b, (((1,), (1,)), ((), ())), preferred_element_type=F32)


def _const_spec(shape):
    nd = len(shape)
    return pl.BlockSpec(shape, lambda *_: (0,) * nd, pipeline_mode=pl.Buffered(1))


_N_GQ, _N_GK, _N_GV, _N_OG, _N_DK, _N_GATE = 0, 512, 1024, 2048, 3072, 4096
_N_END = 6144
_T_GV, _T_DQ, _T_DV, _T_END = 0, 1024, 2048, 3072


def _inproj_body(x_ref, pmw_ref, wnat_ref, wlow_ref, wgk_ref, bgk_ref, bgate_ref, wt_ref,
                 cn_ref, s1_ref, s2_ref, ct_ref, st_ref,
                 gq_ref, gk_ref, gv_ref, og_ref, la_ref, dk_ref, gates_ref,
                 gvt_ref, dqt_ref, dvt_ref):
    u = _rms(x_ref[...], pmw_ref[...]).astype(BF16)

    def mm(lo, hi):
        return _dot(u, wnat_ref[:, lo:hi])

    gq_ref[...] = (mm(_N_GQ, _N_GK) * (GLA_DK ** -0.5)).astype(BF16)
    gk_ref[...] = mm(_N_GK, _N_GV).astype(BF16)
    gv_ref[...] = mm(_N_GV, _N_OG).astype(BF16)
    og = mm(_N_OG, _N_DK)
    og_ref[...] = (og * jax.nn.sigmoid(og)).astype(BF16)

    g_low = _dot(u, wlow_ref[...]).astype(BF16)
    z = _dot(g_low, wgk_ref[...]) + bgk_ref[...]
    la_ref[...] = (jnp.minimum(z, 0.0) - jnp.log(1.0 + jnp.exp(-jnp.abs(z)))) * (1.0 / GLA_GATE_NORM)

    dk = mm(_N_DK, _N_GATE)
    c, s1, s2 = cn_ref[...], s1_ref[...], s2_ref[...]
    for h in range(DIFF_HEADS):
        t = dk[:, h * LANES:(h + 1) * LANES]
        r = t * c + pltpu.roll(t, LANES - ROT_HALF, 1) * s1 + pltpu.roll(t, ROT_HALF, 1) * s2
        dk_ref[:, h * LANES:(h + 1) * LANES] = r.astype(BF16)

    gates_ref[...] = jax.nn.sigmoid(mm(_N_GATE, _N_END) + bgate_ref[...]).astype(BF16)

    def mmt(lo, hi):
        return _dot_nt(wt_ref[lo:hi, :], u)

    gvt_ref[0] = mmt(_T_GV, _T_DQ).astype(BF16)
    dvt_ref[0] = mmt(_T_DV, _T_END).astype(BF16)
    dq = mmt(_T_DQ, _T_DV) * (DIFF_DH ** -0.5 * LOG2E)
    ct, st = ct_ref[0], st_ref[0]
    pieces = []
    for g in range(DIFF_HEADS * 2):
        b = g * DIFF_DH
        t1, t2 = dq[b:b + ROT_HALF], dq[b + ROT_HALF:b + ROT_DIM]
        pieces += [t1 * ct - t2 * st, t2 * ct + t1 * st, dq[b + ROT_DIM:b + DIFF_DH]]
    dqt_ref[0] = jnp.concatenate(pieces, axis=0).astype(BF16)


def _inproj(x2, pmw, wnat, wlow, wgk, bgk, bgate, wt, cn, s1, s2, ct, st):
    T = x2.shape[0]
    tm = TOK_BLK
    nblk = T // tm
    row = lambda i: (i, 0)
    blk3 = lambda i: (i, 0, 0)
    in_specs = [
        pl.BlockSpec((tm, D_MODEL), row),
        _const_spec((1, D_MODEL)),
        _const_spec((D_MODEL, _N_END)),
        _const_spec((D_MODEL, LANES)),
        _const_spec((LANES, GLA_QK)),
        _const_spec((1, GLA_QK)),
        _const_spec((1, 2 * D_MODEL)),
        _const_spec((_T_END, D_MODEL)),
        pl.BlockSpec((tm, LANES), row),
        pl.BlockSpec((tm, LANES), row),
        pl.BlockSpec((tm, LANES), row),
        pl.BlockSpec((1, ROT_HALF, tm), blk3),
        pl.BlockSpec((1, ROT_HALF, tm), blk3),
    ]
    out_shape = [
        jax.ShapeDtypeStruct((T, GLA_QK), BF16),
        jax.ShapeDtypeStruct((T, GLA_QK), BF16),
        jax.ShapeDtypeStruct((T, GLA_V), BF16),
        jax.ShapeDtypeStruct((T, GLA_V), BF16),
        jax.ShapeDtypeStruct((T, GLA_QK), F32),
        jax.ShapeDtypeStruct((T, DIFF_QK), BF16),
        jax.ShapeDtypeStruct((T, 2 * D_MODEL), BF16),
        jax.ShapeDtypeStruct((nblk, GLA_V, tm), BF16),
        jax.ShapeDtypeStruct((nblk, DIFF_QK, tm), BF16),
        jax.ShapeDtypeStruct((nblk, DIFF_V, tm), BF16),
    ]
    out_specs = [
        pl.BlockSpec((tm, GLA_QK), row),
        pl.BlockSpec((tm, GLA_QK), row),
        pl.BlockSpec((tm, GLA_V), row),
        pl.BlockSpec((tm, GLA_V), row),
        pl.BlockSpec((tm, GLA_QK), row),
        pl.BlockSpec((tm, DIFF_QK), row),
        pl.BlockSpec((tm, 2 * D_MODEL), row),
        pl.BlockSpec((1, GLA_V, tm), blk3),
        pl.BlockSpec((1, DIFF_QK, tm), blk3),
        pl.BlockSpec((1, DIFF_V, tm), blk3),
    ]
    return pl.pallas_call(
        _inproj_body,
        grid=(nblk,),
        in_specs=in_specs,
        out_specs=out_specs,
        out_shape=out_shape,
        compiler_params=pltpu.CompilerParams(
            dimension_semantics=("parallel",), vmem_limit_bytes=VMEM_LIMIT),
        name="inproj",
    )(x2, pmw, wnat, wlow, wgk, bgk, bgate, wt, cn, s1, s2, ct, st)


def _gla_body(q_ref, k_ref, v_ref, vt_ref, la_ref, og_ref, nw_ref, o_ref, state_ref):
    @pl.when(pl.program_id(1) == 0)
    def _():
        state_ref[...] = jnp.zeros_like(state_ref)

    n = GLA_SUPER
    ri = lax.broadcasted_iota(jnp.int32, (n, n), 0)
    ci = lax.broadcasted_iota(jnp.int32, (n, n), 1)
    tri = (ri >= ci).astype(BF16)
    same_chunk_causal = ((ri >= GLA_CHUNK) == (ci >= GLA_CHUNK)) & (ci <= ri)
    row = lax.broadcasted_iota(jnp.int32, (n, GLA_DK), 0)
    second = row >= GLA_CHUNK
    nw = nw_ref[...]

    for sc in range(TOK_BLK // n):
        r0 = sc * n
        la = la_ref[r0:r0 + n, :]
        hi = la.astype(BF16)
        rem = la - hi.astype(F32)
        mid = rem.astype(BF16)
        lo = (rem - mid.astype(F32)).astype(BF16)
        g_all = _dot(tri, hi) + _dot(tri, mid) + _dot(tri, lo)

        for h in range(GLA_HEADS):
            ks = slice(h * GLA_DK, (h + 1) * GLA_DK)
            vs = slice(h * GLA_DV, (h + 1) * GLA_DV)
            g = g_all[:, ks]
            q = q_ref[r0:r0 + n, ks].astype(F32)
            k = k_ref[r0:r0 + n, ks].astype(F32)
            v = v_ref[r0:r0 + n, vs]
            vt = vt_ref[0, vs, r0:r0 + n]
            g_mid = jnp.where(second, g[GLA_CHUNK + GLA_CHUNK // 2:GLA_CHUNK + GLA_CHUNK // 2 + 1],
                              g[GLA_CHUNK // 2:GLA_CHUNK // 2 + 1])
            g_c0 = g[GLA_CHUNK - 1:GLA_CHUNK]
            g_last = g[n - 1:n]

            qi = (q * jnp.exp(g - g_mid)).astype(BF16)
            kj = (k * jnp.exp(g_mid - g)).astype(BF16)
            a_diag = _dot_nt(qi, kj)
            qo = jnp.where(second, q * jnp.exp(jnp.minimum(g - g_c0, 0.0)), 0.0).astype(BF16)
            ko = jnp.where(second, 0.0, k * jnp.exp(jnp.minimum(g_c0 - g, 0.0))).astype(BF16)
            a = jnp.where(same_chunk_causal, a_diag, 0.0) + _dot_nt(qo, ko)
            o = _dot(a.astype(BF16), v)

            st = state_ref[h]
            o = o + _dot_nt((q * jnp.exp(g)).astype(BF16), st.astype(BF16))
            k_st = (k * jnp.exp(g_last - g)).astype(BF16)
            state_ref[h] = st * jnp.exp(g_last) + _dot(vt, k_st)

            og = og_ref[r0:r0 + n, vs].astype(F32)
            o_ref[r0:r0 + n, vs] = (_rms(o, nw) * og).astype(BF16)


def _gla(gq, gk, gv, gvt, la, og, nw, B, S):
    T = B * S
    tb = TOK_BLK
    nb = S // tb
    row = lambda b, i: (b * nb + i, 0)
    return pl.pallas_call(
        _gla_body,
        grid=(B, nb),
        in_specs=[
            pl.BlockSpec((tb, GLA_QK), row),
            pl.BlockSpec((tb, GLA_QK), row),
            pl.BlockSpec((tb, GLA_V), row),
            pl.BlockSpec((1, GLA_V, tb), lambda b, i: (b * nb + i, 0, 0)),
            pl.BlockSpec((tb, GLA_QK), row),
            pl.BlockSpec((tb, GLA_V), row),
            pl.BlockSpec((1, GLA_DV), lambda b, i: (0, 0)),
        ],
        out_specs=pl.BlockSpec((tb, GLA_V), row),
        out_shape=jax.ShapeDtypeStruct((T, GLA_V), BF16),
        scratch_shapes=[pltpu.VMEM((GLA_HEADS, GLA_DV, GLA_DK), F32)],
        compiler_params=pltpu.CompilerParams(
            dimension_semantics=("parallel", "arbitrary"), vmem_limit_bytes=VMEM_LIMIT),
        name="gla",
    )(gq, gk, gv, gvt, la, og, nw)


def _attn_body(qt_ref, k_ref, vt_ref, lq1_ref, lk1_ref, lq2_ref, lk2_ref, nw_ref, o_ref,
               acc_ref, s_ref, p_ref, *, lam_init):
    i = pl.program_id(2)
    blk = TOK_BLK
    chains = [(hh, sub) for hh in range(ATT_HPS) for sub in range(2)]
    nc = len(chains)

    qs = []
    for hh in range(ATT_HPS):
        qt = qt_ref[hh]
        frow = lax.broadcasted_iota(jnp.int32, qt.shape, 0)
        qs.append((jnp.where(frow < DIFF_DH, qt, jnp.zeros_like(qt)),
                   jnp.where(frow >= DIFF_DH, qt, jnp.zeros_like(qt))))
    ones_rows = jnp.ones((ATT_ONES_ROWS, blk), BF16)

    def issue_scores(jb, c):
        hh, sub = chains[c]
        kb = k_ref[pl.ds(pl.multiple_of(jb * blk, blk), blk), hh * 2 * DIFF_DH:(hh + 1) * 2 * DIFF_DH]
        s_ref[c] = _dot(kb, qs[hh][sub])

    def issue_values(jb, alphas):
        for c, (hh, sub) in enumerate(chains):
            vx = jnp.concatenate([vt_ref[jb, hh], ones_rows], axis=0)
            acc_ref[c] = acc_ref[c] * alphas[c] + _dot(vx, p_ref[c])

    def softmax(c, m, masked):
        s = s_ref[c]
        if masked:
            kr = lax.broadcasted_iota(jnp.int32, s.shape, 0)
            qc = lax.broadcasted_iota(jnp.int32, s.shape, 1)
            s = jnp.where(kr <= qc, s, NEG_BIG)
        m_new = jnp.maximum(m, jnp.max(s, axis=0, keepdims=True))
        p_ref[c] = jnp.exp2(s - m_new).astype(BF16)
        return m_new, jnp.exp2(m - m_new)

    acc_ref[...] = jnp.zeros_like(acc_ref)
    p_ref[...] = jnp.zeros_like(p_ref)
    for c in range(nc):
        issue_scores(0, c)

    def body(j, carry):
        ms, alphas = carry
        issue_values(jnp.maximum(j - 1, 0), alphas)
        out = []
        for c in range(nc):
            out.append(softmax(c, ms[c], False))
            issue_scores(j + 1, c)
        return tuple(o[0] for o in out), tuple(o[1] for o in out)

    neg = jnp.full((1, blk), NEG_BIG, F32)
    one = jnp.ones((1, blk), F32)
    ms, alphas = lax.fori_loop(0, i, body, ((neg,) * nc, (one,) * nc))

    issue_values(jnp.maximum(i - 1, 0), alphas)
    out = [softmax(c, ms[c], True) for c in range(nc)]
    issue_values(i, tuple(o[1] for o in out))

    lam = (jnp.exp(jnp.sum(lq1_ref[...] * lk1_ref[...], axis=1, keepdims=True))
           - jnp.exp(jnp.sum(lq2_ref[...] * lk2_ref[...], axis=1, keepdims=True)) + lam_init)
    for hh in range(ATT_HPS):
        a1, a2 = acc_ref[2 * hh], acc_ref[2 * hh + 1]
        o = (a1[:DIFF_DV] / a1[DIFF_DV:DIFF_DV + 1]
             - lam * (a2[:DIFF_DV] / a2[DIFF_DV:DIFF_DV + 1]))
        msq = jnp.mean(o * o, axis=0, keepdims=True)
        o = o * lax.rsqrt(msq + EPS) * nw_ref[...] * (1.0 - lam_init)
        o_ref[:, hh * DIFF_DV:(hh + 1) * DIFF_DV] = o.T.astype(BF16)


def _attn(dqt, dk, dvt, lq1, lk1, lq2, lk2, nw_col, B, S, lam_init):
    T = B * S
    blk = TOK_BLK
    nblk = S // blk
    H, hps = DIFF_HEADS, ATT_HPS
    qt5 = dqt.reshape(B, nblk, H, 2 * DIFF_DH, blk)
    vt5 = dvt.reshape(B, nblk, H, DIFF_DV, blk)
    k3 = dk.reshape(B, S, DIFF_QK)
    vec = pl.BlockSpec((1, DIFF_DH), lambda b, h, i: (0, 0))
    out = pl.pallas_call(
        functools.partial(_attn_body, lam_init=lam_init),
        grid=(B, H // hps, nblk),
        in_specs=[
            pl.BlockSpec((None, None, hps, 2 * DIFF_DH, blk), lambda b, h, i: (b, i, h, 0, 0)),
            pl.BlockSpec((None, S, hps * 2 * DIFF_DH), lambda b, h, i: (b, 0, h)),
            pl.BlockSpec((None, nblk, hps, DIFF_DV, blk), lambda b, h, i: (b, 0, h, 0, 0)),
            vec, vec, vec, vec,
            pl.BlockSpec((DIFF_DV, 1), lambda b, h, i: (0, 0)),
        ],
        out_specs=pl.BlockSpec((None, blk, hps * DIFF_DV), lambda b, h, i: (b, i, h)),
        out_shape=jax.ShapeDtypeStruct((B, S, DIFF_V), BF16),
        scratch_shapes=[pltpu.VMEM((hps * 2, DIFF_DV + ATT_ONES_ROWS, blk), F32),
                        pltpu.VMEM((hps * 2, blk, blk), F32),
                        pltpu.VMEM((hps * 2, blk, blk), BF16)],
        compiler_params=pltpu.CompilerParams(
            dimension_semantics=("parallel", "parallel", "arbitrary"), vmem_limit_bytes=VMEM_LIMIT),
        name="attn",
    )(qt5, k3, vt5, lq1, lk1, lq2, lk2, nw_col)
    return out.reshape(T, DIFF_V)


def _merge_body(x_ref, oa_ref, ob_ref, ga_ref, gb_ref, wa_ref, wb_ref, wo_ref, pw_ref, o_ref):
    ya = _dot(oa_ref[...], wa_ref[...])
    yb = _dot(ob_ref[...], wb_ref[...])
    mix = ga_ref[...].astype(F32) * ya + gb_ref[...].astype(F32) * yb
    mixed = _dot(mix.astype(BF16), wo_ref[...])
    o_ref[...] = x_ref[...] + _rms(mixed, pw_ref[...])


def _merge(x2, oa, ob, gates, wa, wb, wo, pw):
    T = x2.shape[0]
    tm = MERGE_TM
    row = lambda i: (i, 0)
    return pl.pallas_call(
        _merge_body,
        grid=(T // tm,),
        in_specs=[
            pl.BlockSpec((tm, D_MODEL), row),
            pl.BlockSpec((tm, GLA_V), row),
            pl.BlockSpec((tm, DIFF_V), row),
            pl.BlockSpec((tm, D_MODEL), lambda i: (i, 0)),
            pl.BlockSpec((tm, D_MODEL), lambda i: (i, 1)),
            _const_spec((GLA_V, D_MODEL)),
            _const_spec((DIFF_V, D_MODEL)),
            _const_spec((D_MODEL, D_MODEL)),
            _const_spec((1, D_MODEL)),
        ],
        out_specs=pl.BlockSpec((tm, D_MODEL), row),
        out_shape=jax.ShapeDtypeStruct((T, D_MODEL), F32),
        compiler_params=pltpu.CompilerParams(
            dimension_semantics=("parallel",), vmem_limit_bytes=VMEM_LIMIT),
        name="merge",
    )(x2, oa, ob, gates, gates, wa, wb, wo, pw)


def _mlp_body(x_ref, wu_ref, wd_ref, prew_ref, postw_ref, o_ref):
    x = x_ref[...]
    h = _rms(x, prew_ref[...]).astype(BF16)
    f = None
    for c in range(D_FF // FF_CHUNK):
        a = jnp.maximum(_dot(h, wu_ref[:, c * FF_CHUNK:(c + 1) * FF_CHUNK]), 0.0)
        part = _dot((a * a).astype(BF16), wd_ref[c * FF_CHUNK:(c + 1) * FF_CHUNK, :])
        f = part if f is None else f + part
    o_ref[...] = x + _rms(f, postw_ref[...])


def _mlp(x1, wu, wd, prew, postw):
    T = x1.shape[0]
    tm = MLP_TM
    row = lambda i: (i, 0)
    return pl.pallas_call(
        _mlp_body,
        grid=(T // tm,),
        in_specs=[
            pl.BlockSpec((tm, D_MODEL), row),
            _const_spec((D_MODEL, D_FF)),
            _const_spec((D_FF, D_MODEL)),
            _const_spec((1, D_MODEL)),
            _const_spec((1, D_MODEL)),
        ],
        out_specs=pl.BlockSpec((tm, D_MODEL), row),
        out_shape=jax.ShapeDtypeStruct((T, D_MODEL), F32),
        compiler_params=pltpu.CompilerParams(
            dimension_semantics=("parallel",), vmem_limit_bytes=VMEM_LIMIT),
        name="mlp",
    )(x1, wu, wd, prew, postw)


def _rope_tables(positions):
    T = positions.size
    inv_freq = ROPE_THETA ** (-jnp.arange(0, ROT_DIM, 2, dtype=F32) / ROT_DIM)
    ang = positions.astype(F32).reshape(T, 1) * inv_freq
    cos, sin = jnp.cos(ang), jnp.sin(ang)
    pad = DIFF_DH - ROT_DIM
    one = jnp.ones((T, pad), F32)
    zero = jnp.zeros((T, pad), F32)
    z8 = jnp.zeros((T, ROT_HALF), F32)
    cn = jnp.tile(jnp.concatenate([cos, cos, one], axis=1), (1, 2))
    s1 = jnp.tile(jnp.concatenate([-sin, z8, zero], axis=1), (1, 2))
    s2 = jnp.tile(jnp.concatenate([z8, sin, zero], axis=1), (1, 2))
    ct = cos.reshape(T // TOK_BLK, TOK_BLK, ROT_HALF).transpose(0, 2, 1)
    st = sin.reshape(T // TOK_BLK, TOK_BLK, ROT_HALF).transpose(0, 2, 1)
    return cn, s1, s2, ct, st


def kernel(x, positions, w_in, b_gate, w_gk_up, b_gk, gla_norm_w, lambda_q1, lambda_k1, lambda_q2,
           lambda_k2, diff_norm_w, w_branch_a, w_branch_b, w_out, pre_mix_w, post_mix_w, pre_mlp_w,
           post_mlp_w, w_up, w_down):
    B, S, D = x.shape
    T = B * S
    depth = w_in.shape[0]
    cn, s1, s2, ct, st = _rope_tables(positions)
    offs = [0]
    for n in IN_SIZES:
        offs.append(offs[-1] + n)
    (o_gq, o_gk, o_gv, o_og, o_low, o_dq, o_dk, o_dv, o_gates, o_end) = offs

    x2 = x.reshape(T, D)
    for l in range(depth):
        lam_init = 0.8 - 0.6 * math.exp(-0.3 * l)
        w = w_in[l]
        wnat = jnp.concatenate(
            [w[:, o_gq:o_gk], w[:, o_gk:o_gv], w[:, o_gv:o_og], w[:, o_og:o_low],
             w[:, o_dk:o_dv], w[:, o_gates:o_end]], axis=1).astype(BF16)
        wlow = jnp.pad(w[:, o_low:o_dq], ((0, 0), (0, LANES - GLA_RANK))).astype(BF16)
        wgk = jnp.pad(w_gk_up[l], ((0, LANES - GLA_RANK), (0, 0))).astype(BF16)
        wt = jnp.concatenate([w[:, o_gv:o_og], w[:, o_dq:o_dk], w[:, o_dv:o_gates]], axis=1).T.astype(BF16)

        gq, gk, gv, og, la, dk, gates, gvt, dqt, dvt = _inproj(
            x2, pre_mix_w[l].reshape(1, D), wnat, wlow, wgk, b_gk[l].reshape(1, -1),
            b_gate[l].reshape(1, -1), wt, cn, s1, s2, ct, st)

        o_a = _gla(gq, gk, gv, gvt, la, og, gla_norm_w[l].reshape(1, -1), B, S)
        o_b = _attn(dqt, dk, dvt,
                    lambda_q1[l].reshape(1, -1), lambda_k1[l].reshape(1, -1),
                    lambda_q2[l].reshape(1, -1), lambda_k2[l].reshape(1, -1),
                    diff_norm_w[l].reshape(-1, 1), B, S, lam_init)

        x1 = _merge(x2, o_a, o_b, gates, w_branch_a[l].astype(BF16), w_branch_b[l].astype(BF16),
                    w_out[l].astype(BF16), post_mix_w[l].reshape(1, D))
        x2 = _mlp(x1, w_up[l].astype(BF16), w_down[l].astype(BF16),
                  pre_mlp_w[l].reshape(1, D), post_mlp_w[l].reshape(1, D))
    return x2.reshape(B, S, D)
```

```python
import functools
import math

import jax
import jax.numpy as jnp
from jax import lax
from jax.experimental import pallas as pl
from jax.experimental.pallas import tpu as pltpu

F32 = jnp.float32
BF16 = jnp.bfloat16

D_MODEL = 1024
GLA_HEADS = 4
GLA_DK = 128
GLA_DV = 256
GLA_RANK = 16
GLA_GATE_NORM = 16.0
GLA_CHUNK = 64
DIFF_HEADS = 8
DIFF_DH = 64
DIFF_DV = 2 * DIFF_DH
ROT_DIM = DIFF_DH // 4
ROT_HALF = ROT_DIM // 2
ROPE_THETA = 500000.0
D_FF = 4 * D_MODEL
EPS = 1e-6

GLA_QK = GLA_HEADS * GLA_DK
GLA_V = GLA_HEADS * GLA_DV
DIFF_QK = DIFF_HEADS * 2 * DIFF_DH
DIFF_V = DIFF_HEADS * DIFF_DV
IN_SIZES = (GLA_QK, GLA_QK, GLA_V, GLA_V, GLA_RANK, DIFF_QK, DIFF_QK, DIFF_V, 2 * D_MODEL)

LANES = 128
TOK_BLK = 256
GLA_SUPER = 2 * GLA_CHUNK
ATT_HPS = 4
ATT_ONES_ROWS = 16
LOG2E = math.log2(math.e)
MERGE_TM = 512
MLP_TM = 512
FF_CHUNK = 1024
VMEM_LIMIT = 56 * 1024 * 1024
NEG_BIG = -1e30


def _rms(xf, w):
    ms = jnp.mean(xf * xf, axis=-1, keepdims=True)
    return xf * lax.rsqrt(ms + EPS) * w


def _dot(a, b):
    return jnp.dot(a, b, preferred_element_type=F32)


def _dot_nt(a, b):
    return lax.dot_general(a, b, (((1,), (1,)), ((), ())), preferred_element_type=F32)


def _const_spec(shape):
    nd = len(shape)
    return pl.BlockSpec(shape, lambda *_: (0,) * nd, pipeline_mode=pl.Buffered(1))


_N_GQ, _N_GK, _N_GV, _N_OG, _N_DK, _N_GATE = 0, 512, 1024, 2048, 3072, 4096
_N_END = 6144
_T_DQ, _T_DV, _T_END = 0, 1024, 2048


def _inproj_body(x_ref, pmw_ref, wnat_ref, wlow_ref, wgk_ref, bgk_ref, bgate_ref, wt_ref,
                 cs_ref, rope_e_ref, ct_ref, st_ref,
                 gq_ref, gk_ref, gv_ref, og_ref, la_ref, dk_ref, gates_ref,
                 dqt_ref, dvt_ref):
    u = _rms(x_ref[...], pmw_ref[...]).astype(BF16)

    def mm(lo, hi):
        return _dot(u, wnat_ref[:, lo:hi])

    gq_ref[...] = (mm(_N_GQ, _N_GK) * (GLA_DK ** -0.5)).astype(BF16)
    gk_ref[...] = mm(_N_GK, _N_GV).astype(BF16)
    gv_ref[...] = mm(_N_GV, _N_OG).astype(BF16)
    og = mm(_N_OG, _N_DK)
    og_ref[...] = (og * jax.nn.sigmoid(og)).astype(BF16)

    g_low = _dot(u, wlow_ref[...]).astype(BF16)
    z = _dot(g_low, wgk_ref[...]) + bgk_ref[...]
    la_ref[...] = (jnp.minimum(z, 0.0) - jnp.log(1.0 + jnp.exp(-jnp.abs(z)))) * (1.0 / GLA_GATE_NORM)

    dk = mm(_N_DK, _N_GATE)
    cs = cs_ref[...]
    cs_hi = cs.astype(BF16)
    cs_lo = (cs - cs_hi.astype(F32)).astype(BF16)
    tab = _dot(cs_hi, rope_e_ref[...]) + _dot(cs_lo, rope_e_ref[...])
    lane = lax.broadcasted_iota(jnp.int32, (1, LANES), 1)
    c = tab[:, :LANES] + jnp.where(lane % DIFF_DH >= ROT_DIM, 1.0, 0.0)
    s1, s2 = tab[:, LANES:2 * LANES], tab[:, 2 * LANES:]
    for h in range(DIFF_HEADS):
        t = dk[:, h * LANES:(h + 1) * LANES]
        r = t * c + pltpu.roll(t, LANES - ROT_HALF, 1) * s1 + pltpu.roll(t, ROT_HALF, 1) * s2
        dk_ref[:, h * LANES:(h + 1) * LANES] = r.astype(BF16)

    gates_ref[...] = jax.nn.sigmoid(mm(_N_GATE, _N_END) + bgate_ref[...]).astype(BF16)

    def mmt(lo, hi):
        return _dot_nt(wt_ref[lo:hi, :], u)

    dvt_ref[0] = mmt(_T_DV, _T_END).astype(BF16)
    dq = mmt(_T_DQ, _T_DV) * (DIFF_DH ** -0.5 * LOG2E)
    ct, st = ct_ref[0], st_ref[0]
    pieces = []
    for g in range(DIFF_HEADS * 2):
        b = g * DIFF_DH
        t1, t2 = dq[b:b + ROT_HALF], dq[b + ROT_HALF:b + ROT_DIM]
        pieces += [t1 * ct - t2 * st, t2 * ct + t1 * st, dq[b + ROT_DIM:b + DIFF_DH]]
    dqt_ref[0] = jnp.concatenate(pieces, axis=0).astype(BF16)


def _inproj(x2, pmw, wnat, wlow, wgk, bgk, bgate, wt, cs, rope_e, ct, st):
    T = x2.shape[0]
    tm = TOK_BLK
    nblk = T // tm
    row = lambda i: (i, 0)
    blk3 = lambda i: (i, 0, 0)
    in_specs = [
        pl.BlockSpec((tm, D_MODEL), row),
        _const_spec((1, D_MODEL)),
        _const_spec((D_MODEL, _N_END)),
        _const_spec((D_MODEL, LANES)),
        _const_spec((LANES, GLA_QK)),
        _const_spec((1, GLA_QK)),
        _const_spec((1, 2 * D_MODEL)),
        _const_spec((_T_END, D_MODEL)),
        pl.BlockSpec((tm, 2 * ROT_HALF), row),
        _const_spec((2 * ROT_HALF, 3 * LANES)),
        pl.BlockSpec((1, ROT_HALF, tm), blk3),
        pl.BlockSpec((1, ROT_HALF, tm), blk3),
    ]
    out_shape = [
        jax.ShapeDtypeStruct((T, GLA_QK), BF16),
        jax.ShapeDtypeStruct((T, GLA_QK), BF16),
        jax.ShapeDtypeStruct((T, GLA_V), BF16),
        jax.ShapeDtypeStruct((T, GLA_V), BF16),
        jax.ShapeDtypeStruct((T, GLA_QK), F32),
        jax.ShapeDtypeStruct((T, DIFF_QK), BF16),
        jax.ShapeDtypeStruct((T, 2 * D_MODEL), BF16),
        jax.ShapeDtypeStruct((nblk, DIFF_QK, tm), BF16),
        jax.ShapeDtypeStruct((nblk, DIFF_V, tm), BF16),
    ]
    out_specs = [
        pl.BlockSpec((tm, GLA_QK), row),
        pl.BlockSpec((tm, GLA_QK), row),
        pl.BlockSpec((tm, GLA_V), row),
        pl.BlockSpec((tm, GLA_V), row),
        pl.BlockSpec((tm, GLA_QK), row),
        pl.BlockSpec((tm, DIFF_QK), row),
        pl.BlockSpec((tm, 2 * D_MODEL), row),
        pl.BlockSpec((1, DIFF_QK, tm), blk3),
        pl.BlockSpec((1, DIFF_V, tm), blk3),
    ]
    return pl.pallas_call(
        _inproj_body,
        grid=(nblk,),
        in_specs=in_specs,
        out_specs=out_specs,
        out_shape=out_shape,
        compiler_params=pltpu.CompilerParams(
            dimension_semantics=("parallel",), vmem_limit_bytes=VMEM_LIMIT),
        name="inproj",
    )(x2, pmw, wnat, wlow, wgk, bgk, bgate, wt, cs, rope_e, ct, st)


def _gla_body(q_ref, k_ref, v_ref, la_ref, og_ref, nw_ref, o_ref, state_ref):
    @pl.when(pl.program_id(1) == 0)
    def _():
        state_ref[...] = jnp.zeros_like(state_ref)

    n = GLA_SUPER
    ri = lax.broadcasted_iota(jnp.int32, (n, n), 0)
    ci = lax.broadcasted_iota(jnp.int32, (n, n), 1)
    tri = (ri >= ci).astype(BF16)
    same_chunk_causal = ((ri >= GLA_CHUNK) == (ci >= GLA_CHUNK)) & (ci <= ri)
    row = lax.broadcasted_iota(jnp.int32, (n, GLA_DK), 0)
    second = row >= GLA_CHUNK
    nw = nw_ref[...]

    for sc in range(TOK_BLK // n):
        r0 = sc * n
        la = la_ref[r0:r0 + n, :]
        hi = la.astype(BF16)
        lo = (la - hi.astype(F32)).astype(BF16)
        g_all = _dot(tri, hi) + _dot(tri, lo)

        pend = []
        for h in range(GLA_HEADS):
            ks = slice(h * GLA_DK, (h + 1) * GLA_DK)
            vs = slice(h * GLA_DV, (h + 1) * GLA_DV)
            g = g_all[:, ks]
            q = q_ref[r0:r0 + n, ks].astype(F32)
            k = k_ref[r0:r0 + n, ks].astype(F32)
            v = v_ref[r0:r0 + n, vs]
            g_mid = jnp.where(second, g[GLA_CHUNK + GLA_CHUNK // 2:GLA_CHUNK + GLA_CHUNK // 2 + 1],
                              g[GLA_CHUNK // 2:GLA_CHUNK // 2 + 1])
            g_c0 = g[GLA_CHUNK - 1:GLA_CHUNK]
            g_last = g[n - 1:n]

            qi = (q * jnp.exp(g - g_mid)).astype(BF16)
            kj = (k * jnp.exp(g_mid - g)).astype(BF16)
            a_diag = _dot_nt(qi, kj)
            qo = jnp.where(second, q * jnp.exp(jnp.minimum(g - g_c0, 0.0)), 0.0).astype(BF16)
            ko = jnp.where(second, 0.0, k * jnp.exp(jnp.minimum(g_c0 - g, 0.0))).astype(BF16)
            a = (jnp.where(same_chunk_causal, a_diag, 0.0) + _dot_nt(qo, ko)).astype(BF16)

            st = state_ref[h]
            o_inter = _dot((q * jnp.exp(g)).astype(BF16), st.astype(BF16))
            gt = g.T
            g_last_col = gt[:, n - 1:n]
            k_st_t = (k.T * jnp.exp(g_last_col - gt)).astype(BF16)
            state_ref[h] = st * jnp.exp(g_last_col) + _dot(k_st_t, v)
            pend.append((a, v, o_inter, vs))

        for a, v, o_inter, vs in pend:
            o = _dot(a, v) + o_inter
            og = og_ref[r0:r0 + n, vs].astype(F32)
            o_ref[r0:r0 + n, vs] = (_rms(o, nw) * og).astype(BF16)


def _gla(gq, gk, gv, la, og, nw, B, S):
    T = B * S
    tb = TOK_BLK
    nb = S // tb
    row = lambda b, i: (b * nb + i, 0)
    return pl.pallas_call(
        _gla_body,
        grid=(B, nb),
        in_specs=[
            pl.BlockSpec((tb, GLA_QK), row),
            pl.BlockSpec((tb, GLA_QK), row),
            pl.BlockSpec((tb, GLA_V), row),
            pl.BlockSpec((tb, GLA_QK), row),
            pl.BlockSpec((tb, GLA_V), row),
            pl.BlockSpec((1, GLA_DV), lambda b, i: (0, 0)),
        ],
        out_specs=pl.BlockSpec((tb, GLA_V), row),
        out_shape=jax.ShapeDtypeStruct((T, GLA_V), BF16),
        scratch_shapes=[pltpu.VMEM((GLA_HEADS, GLA_DK, GLA_DV), F32)],
        compiler_params=pltpu.CompilerParams(
            dimension_semantics=("parallel", "arbitrary"), vmem_limit_bytes=VMEM_LIMIT),
        name="gla",
    )(gq, gk, gv, la, og, nw)


def _attn_body(qt_ref, k_ref, vt_ref, lq1_ref, lk1_ref, lq2_ref, lk2_ref, nw_ref, o_ref,
               acc_ref, s_ref, p_ref, *, lam_init):
    i = pl.program_id(2)
    blk = TOK_BLK
    chains = [(hh, sub) for hh in range(ATT_HPS) for sub in range(2)]
    nc = len(chains)

    qs = []
    for hh in range(ATT_HPS):
        qt = qt_ref[hh]
        frow = lax.broadcasted_iota(jnp.int32, qt.shape, 0)
        qs.append((jnp.where(frow < DIFF_DH, qt, jnp.zeros_like(qt)),
                   jnp.where(frow >= DIFF_DH, qt, jnp.zeros_like(qt))))
    ones_rows = jnp.ones((ATT_ONES_ROWS, blk), BF16)

    def issue_scores(jb, c):
        hh, sub = chains[c]
        kb = k_ref[pl.ds(pl.multiple_of(jb * blk, blk), blk), hh * 2 * DIFF_DH:(hh + 1) * 2 * DIFF_DH]
        s_ref[c] = _dot(kb, qs[hh][sub])

    def issue_values(jb, alphas):
        for c, (hh, sub) in enumerate(chains):
            vx = jnp.concatenate([vt_ref[jb, hh], ones_rows], axis=0)
            acc_ref[c] = acc_ref[c] * alphas[c] + _dot(vx, p_ref[c])

    def softmax(c, m, masked):
        s = s_ref[c]
        if masked:
            kr = lax.broadcasted_iota(jnp.int32, s.shape, 0)
            qc = lax.broadcasted_iota(jnp.int32, s.shape, 1)
            s = jnp.where(kr <= qc, s, NEG_BIG)
        m_new = jnp.maximum(m, jnp.max(s, axis=0, keepdims=True))
        p_ref[c] = jnp.exp2(s - m_new).astype(BF16)
        return m_new, jnp.exp2(m - m_new)

    acc_ref[...] = jnp.zeros_like(acc_ref)
    for c in range(nc):
        issue_scores(0, c)

    crow = lax.broadcasted_iota(jnp.int32, (nc, blk), 0)

    def rows(a):
        return [a[c:c + 1] for c in range(nc)]

    def body(j, carry, with_values):
        m_all, al_all = carry
        if with_values:
            issue_values(j - 1, rows(al_all))
        for c in range(nc):
            m_new, al = softmax(c, m_all[c:c + 1], False)
            m_all = jnp.where(crow == c, m_new, m_all)
            al_all = jnp.where(crow == c, al, al_all)
            issue_scores(j + 1, c)
        return m_all, al_all

    carry = (jnp.full((nc, blk), NEG_BIG, F32), jnp.ones((nc, blk), F32))
    carry = lax.fori_loop(0, jnp.minimum(i, 1), lambda t, cr: body(0, cr, False), carry)
    m_all, al_all = lax.fori_loop(1, i, lambda j, cr: body(j, cr, True), carry)

    @pl.when(i == 0)
    def _():
        p_ref[...] = jnp.zeros_like(p_ref)

    issue_values(jnp.maximum(i - 1, 0), rows(al_all))

    lam =(jnp.exp(jnp.sum(lq1_ref[...] * lk1_ref[...], axis=1, keepdims=True))
           - jnp.exp(jnp.sum(lq2_ref[...] * lk2_ref[...], axis=1, keepdims=True)) + lam_init)
    for hh in range(ATT_HPS):
        vx = jnp.concatenate([vt_ref[i, hh], ones_rows], axis=0)
        for c in (2 * hh, 2 * hh + 1):
            al = softmax(c, m_all[c:c + 1], True)[1]
            acc_ref[c] = acc_ref[c] * al + _dot(vx, p_ref[c])
        a1, a2 = acc_ref[2 * hh], acc_ref[2 * hh + 1]
        o = (a1[:DIFF_DV] / a1[DIFF_DV:DIFF_DV + 1]
             - lam * (a2[:DIFF_DV] / a2[DIFF_DV:DIFF_DV + 1]))
        msq = jnp.mean(o * o, axis=0, keepdims=True)
        o = o * lax.rsqrt(msq + EPS) * nw_ref[...] * (1.0 - lam_init)
        o_ref[:, hh * DIFF_DV:(hh + 1) * DIFF_DV] = o.T.astype(BF16)


def _attn(dqt, dk, dvt, lq1, lk1, lq2, lk2, nw_col, B, S, lam_init):
    T = B * S
    blk = TOK_BLK
    nblk = S // blk
    H, hps = DIFF_HEADS, ATT_HPS
    qt5 = dqt.reshape(B, nblk, H, 2 * DIFF_DH, blk)
    vt5 = dvt.reshape(B, nblk, H, DIFF_DV, blk)
    k3 = dk.reshape(B, S, DIFF_QK)
    vec = pl.BlockSpec((1, DIFF_DH), lambda b, h, i: (0, 0))
    out = pl.pallas_call(
        functools.partial(_attn_body, lam_init=lam_init),
        grid=(B, H // hps, nblk),
        in_specs=[
            pl.BlockSpec((None, None, hps, 2 * DIFF_DH, blk), lambda b, h, i: (b, i, h, 0, 0)),
            pl.BlockSpec((None, S, hps * 2 * DIFF_DH), lambda b, h, i: (b, 0, h)),
            pl.BlockSpec((None, nblk, hps, DIFF_DV, blk), lambda b, h, i: (b, 0, h, 0, 0)),
            vec, vec, vec, vec,
            pl.BlockSpec((DIFF_DV, 1), lambda b, h, i: (0, 0)),
        ],
        out_specs=pl.BlockSpec((None, blk, hps * DIFF_DV), lambda b, h, i: (b, i, h)),
        out_shape=jax.ShapeDtypeStruct((B, S, DIFF_V), BF16),
        scratch_shapes=[pltpu.VMEM((hps * 2, DIFF_DV + ATT_ONES_ROWS, blk), F32),
                        pltpu.VMEM((hps * 2, blk, blk), F32),
                        pltpu.VMEM((hps * 2, blk, blk), BF16)],
        compiler_params=pltpu.CompilerParams(
            dimension_semantics=("parallel", "parallel", "arbitrary"), vmem_limit_bytes=VMEM_LIMIT),
        name="attn",
    )(qt5, k3, vt5, lq1, lk1, lq2, lk2, nw_col)
    return out.reshape(T, DIFF_V)


def _merge_body(x_ref, oa_ref, ob_ref, ga_ref, gb_ref, wa_ref, wb_ref, wo_ref, pw_ref, o_ref):
    ya = _dot(oa_ref[...], wa_ref[...])
    yb = _dot(ob_ref[...], wb_ref[...])
    mix = ga_ref[...].astype(F32) * ya + gb_ref[...].astype(F32) * yb
    mixed = _dot(mix.astype(BF16), wo_ref[...])
    o_ref[...] = x_ref[...] + _rms(mixed, pw_ref[...])


def _merge(x2, oa, ob, gates, wa, wb, wo, pw):
    T = x2.shape[0]
    tm = MERGE_TM
    row = lambda i: (i, 0)
    return pl.pallas_call(
        _merge_body,
        grid=(T // tm,),
        in_specs=[
            pl.BlockSpec((tm, D_MODEL), row),
            pl.BlockSpec((tm, GLA_V), row),
            pl.BlockSpec((tm, DIFF_V), row),
            pl.BlockSpec((tm, D_MODEL), lambda i: (i, 0)),
            pl.BlockSpec((tm, D_MODEL), lambda i: (i, 1)),
            _const_spec((GLA_V, D_MODEL)),
            _const_spec((DIFF_V, D_MODEL)),
            _const_spec((D_MODEL, D_MODEL)),
            _const_spec((1, D_MODEL)),
        ],
        out_specs=pl.BlockSpec((tm, D_MODEL), row),
        out_shape=jax.ShapeDtypeStruct((T, D_MODEL), F32),
        compiler_params=pltpu.CompilerParams(
            dimension_semantics=("parallel",), vmem_limit_bytes=VMEM_LIMIT),
        name="merge",
    )(x2, oa, ob, gates, gates, wa, wb, wo, pw)


def _mlp_body(x_ref, wu_ref, wd_ref, prew_ref, postw_ref, o_ref):
    x = x_ref[...]
    h = _rms(x, prew_ref[...]).astype(BF16)
    f = None
    for c in range(D_FF // FF_CHUNK):
        a = jnp.maximum(_dot(h, wu_ref[:, c * FF_CHUNK:(c + 1) * FF_CHUNK]), 0.0)
        part = _dot((a * a).astype(BF16), wd_ref[c * FF_CHUNK:(c + 1) * FF_CHUNK, :])
        f = part if f is None else f + part
    o_ref[...] = x + _rms(f, postw_ref[...])


def _mlp(x1, wu, wd, prew, postw):
    T = x1.shape[0]
    tm = MLP_TM
    row = lambda i: (i, 0)
    return pl.pallas_call(
        _mlp_body,
        grid=(T // tm,),
        in_specs=[
            pl.BlockSpec((tm, D_MODEL), row),
            _const_spec((D_MODEL, D_FF)),
            _const_spec((D_FF, D_MODEL)),
            _const_spec((1, D_MODEL)),
            _const_spec((1, D_MODEL)),
        ],
        out_specs=pl.BlockSpec((tm, D_MODEL), row),
        out_shape=jax.ShapeDtypeStruct((T, D_MODEL), F32),
        compiler_params=pltpu.CompilerParams(
            dimension_semantics=("parallel",), vmem_limit_bytes=VMEM_LIMIT),
        name="mlp",
    )(x1, wu, wd, prew, postw)


def _rope_tables(positions):
    T = positions.size
    inv_freq = ROPE_THETA ** (-jnp.arange(0, ROT_DIM, 2, dtype=F32) / ROT_DIM)
    ang = positions.astype(F32).reshape(T, 1) * inv_freq
    cos, sin = jnp.cos(ang), jnp.sin(ang)
    cs = jnp.concatenate([cos, sin], axis=1)
    ct = cos.reshape(T // TOK_BLK, TOK_BLK, ROT_HALF).transpose(0, 2, 1)
    st = sin.reshape(T // TOK_BLK, TOK_BLK, ROT_HALF).transpose(0, 2, 1)
    return cs, ct, st


def _rope_expand_matrix():
    import numpy as np
    e = np.zeros((2 * ROT_HALF, 3 * LANES), np.float32)
    for base in range(0, LANES, DIFF_DH):
        for j in range(ROT_HALF):
            e[j, base + j] = 1.0
            e[j, base + ROT_HALF + j] = 1.0
            e[ROT_HALF + j, LANES + base + j] = -1.0
            e[ROT_HALF + j, 2 * LANES + base + ROT_HALF + j] = 1.0
    return jnp.asarray(e, BF16)


def kernel(x, positions, w_in, b_gate, w_gk_up, b_gk, gla_norm_w, lambda_q1, lambda_k1, lambda_q2,
           lambda_k2, diff_norm_w, w_branch_a, w_branch_b, w_out, pre_mix_w, post_mix_w, pre_mlp_w,
           post_mlp_w, w_up, w_down):
    B, S, D = x.shape
    T = B * S
    depth = w_in.shape[0]
    cs, ct, st = _rope_tables(positions)
    rope_e = _rope_expand_matrix()
    offs = [0]
    for n in IN_SIZES:
        offs.append(offs[-1] + n)
    (o_gq, o_gk, o_gv, o_og, o_low, o_dq, o_dk, o_dv, o_gates, o_end) = offs

    x2 = x.reshape(T, D)
    for l in range(depth):
        lam_init = 0.8 - 0.6 * math.exp(-0.3 * l)
        w = w_in[l]
        wnat = jnp.concatenate(
            [w[:, o_gq:o_gk], w[:, o_gk:o_gv], w[:, o_gv:o_og], w[:, o_og:o_low],
             w[:, o_dk:o_dv], w[:, o_gates:o_end]], axis=1).astype(BF16)
        wlow = jnp.pad(w[:, o_low:o_dq], ((0, 0), (0, LANES - GLA_RANK))).astype(BF16)
        wgk = jnp.pad(w_gk_up[l], ((0, LANES - GLA_RANK), (0, 0))).astype(BF16)
        wt = jnp.concatenate([w[:, o_dq:o_dk], w[:, o_dv:o_gates]], axis=1).T.astype(BF16)

        gq, gk, gv, og, la, dk, gates, dqt, dvt = _inproj(
            x2, pre_mix_w[l].reshape(1, D), wnat, wlow, wgk, b_gk[l].reshape(1, -1),
            b_gate[l].reshape(1, -1), wt, cs, rope_e, ct, st)

        o_a = _gla(gq, gk, gv, la, og, gla_norm_w[l].reshape(1, -1), B, S)
        o_b = _attn(dqt, dk, dvt,
                    lambda_q1[l].reshape(1, -1), lambda_k1[l].reshape(1, -1),
                    lambda_q2[l].reshape(1, -1), lambda_k2[l].reshape(1, -1),
                    diff_norm_w[l].reshape(-1, 1), B, S, lam_init)

        x1 = _merge(x2, o_a, o_b, gates, w_branch_a[l].astype(BF16), w_branch_b[l].astype(BF16),
                    w_out[l].astype(BF16), post_mix_w[l].reshape(1, D))
        x2 = _mlp(x1, w_up[l].astype(BF16), w_down[l].astype(BF16),
                  pre_mlp_w[l].reshape(1, D), post_mlp_w[l].reshape(1, D))
    return x2.reshape(B, S, D)
```

```python
import functools
import math

import jax
import jax.numpy as jnp
from jax import lax
from jax.experimental import pallas as pl
from jax.experimental.pallas import tpu as pltpu

F32 = jnp.float32
BF16 = jnp.bfloat16

D_MODEL = 1024
GLA_HEADS = 4
GLA_DK = 128
GLA_DV = 256
GLA_RANK = 16
GLA_GATE_NORM = 16.0
GLA_CHUNK = 64
DIFF_HEADS = 8
DIFF_DH = 64
DIFF_DV = 2 * DIFF_DH
ROT_DIM = DIFF_DH // 4
ROT_HALF = ROT_DIM // 2
ROPE_THETA = 500000.0
D_FF = 4 * D_MODEL
EPS = 1e-6

GLA_QK = GLA_HEADS * GLA_DK
GLA_V = GLA_HEADS * GLA_DV
DIFF_QK = DIFF_HEADS * 2 * DIFF_DH
DIFF_V = DIFF_HEADS * DIFF_DV
IN_SIZES = (GLA_QK, GLA_QK, GLA_V, GLA_V, GLA_RANK, DIFF_QK, DIFF_QK, DIFF_V, 2 * D_MODEL)

LANES = 128
MXU_COLS = 256
TOK_BLK = 256
GLA_SUPER = 2 * GLA_CHUNK
ATT_HPS = 4
ATT_ONES_ROWS = 16
LOG2E = math.log2(math.e)
INPROJ_TM = 512
MERGE_TM = 512
MLP_TM = 512
FF_CHUNK = 1024
VMEM_LIMIT = 56 * 1024 * 1024
NEG_BIG = -1e30


def _rms(xf, w):
    ms = jnp.mean(xf * xf, axis=-1, keepdims=True)
    return xf * lax.rsqrt(ms + EPS) * w


def _dot(a, b):
    return jnp.dot(a, b, preferred_element_type=F32)


def _dot_nt(a, b):
    return lax.dot_general(a, b, (((1,), (1,)), ((), ())), preferred_element_type=F32)


def _const_spec(shape):
    nd = len(shape)
    return pl.BlockSpec(shape, lambda *_: (0,) * nd, pipeline_mode=pl.Buffered(1))


_N_GQ, _N_GK, _N_GV, _N_OG, _N_DK, _N_GATE = 0, 512, 1024, 2048, 3072, 4096
_N_END = 6144
_T_DQ, _T_DV, _T_END = 0, 1024, 2048


def _inproj_body(x_ref, pmw_ref, wnat_ref, wlow_ref, wgk_ref, bgk_ref, bgate_ref, wt_ref,
                 cs_ref, rope_e_ref, ct_ref, st_ref,
                 gq_ref, gk_ref, gv_ref, og_ref, la_ref, dk_ref, gates_ref,
                 dqt_ref, dvt_ref):
    u = _rms(x_ref[...], pmw_ref[...]).astype(BF16)

    def mm(lo, hi):
        return _dot(u, wnat_ref[:, lo:hi])

    gq_ref[...] = (mm(_N_GQ, _N_GK) * (GLA_DK ** -0.5)).astype(BF16)
    gk_ref[...] = mm(_N_GK, _N_GV).astype(BF16)
    gv_ref[...] = mm(_N_GV, _N_OG).astype(BF16)
    og = mm(_N_OG, _N_DK)
    og_ref[...] = (og * jax.nn.sigmoid(og)).astype(BF16)

    g_low = _dot(u, wlow_ref[...]).astype(BF16)
    z = _dot(g_low, wgk_ref[...]) + bgk_ref[...]
    la_ref[...] = (jnp.minimum(z, 0.0) - jnp.log(1.0 + jnp.exp(-jnp.abs(z)))) * (1.0 / GLA_GATE_NORM)

    dk = mm(_N_DK, _N_GATE)
    cs = cs_ref[...]
    cs_hi = cs.astype(BF16)
    cs_lo = (cs - cs_hi.astype(F32)).astype(BF16)
    tab = _dot(cs_hi, rope_e_ref[...]) + _dot(cs_lo, rope_e_ref[...])
    lane = lax.broadcasted_iota(jnp.int32, (1, LANES), 1) % DIFF_DH
    c = tab[:, :LANES] + jnp.where(lane >= ROT_DIM, 1.0, 0.0)
    sgn_sin = tab[:, LANES:]
    first_half = lane < ROT_HALF
    for h in range(DIFF_HEADS):
        t = dk[:, h * LANES:(h + 1) * LANES]
        partner = jnp.where(first_half, pltpu.roll(t, LANES - ROT_HALF, 1), pltpu.roll(t, ROT_HALF, 1))
        dk_ref[:, h * LANES:(h + 1) * LANES] = (t * c + partner * sgn_sin).astype(BF16)

    gates_ref[...] = jax.nn.sigmoid(mm(_N_GATE, _N_END) + bgate_ref[...]).astype(BF16)

    def mmt(lo, hi):
        return _dot_nt(wt_ref[lo:hi, :], u)

    nsub = dvt_ref.shape[0]
    dv = mmt(_T_DV, _T_END).astype(BF16)
    dq = mmt(_T_DQ, _T_DV) * (DIFF_DH ** -0.5 * LOG2E)
    for r in range(nsub):
        tok = slice(r * TOK_BLK, (r + 1) * TOK_BLK)
        dvt_ref[r] = dv[:, tok]
        ct, st = ct_ref[r], st_ref[r]
        pieces = []
        for g in range(DIFF_HEADS * 2):
            b = g * DIFF_DH
            t1, t2 = dq[b:b + ROT_HALF, tok], dq[b + ROT_HALF:b + ROT_DIM, tok]
            pieces += [t1 * ct - t2 * st, t2 * ct + t1 * st, dq[b + ROT_DIM:b + DIFF_DH, tok]]
        dqt_ref[r] = jnp.concatenate(pieces, axis=0).astype(BF16)


def _inproj(x2, pmw, wnat, wlow, wgk, bgk, bgate, wt, cs, rope_e, ct, st):
    T = x2.shape[0]
    tm = INPROJ_TM
    blk = TOK_BLK
    nsub = tm // blk
    nblk = T // blk
    row = lambda i: (i, 0)
    blk3 = lambda i: (i, 0, 0)
    in_specs = [
        pl.BlockSpec((tm, D_MODEL), row),
        _const_spec((1, D_MODEL)),
        _const_spec((D_MODEL, _N_END)),
        _const_spec((D_MODEL, MXU_COLS)),
        _const_spec((MXU_COLS, GLA_QK)),
        _const_spec((1, GLA_QK)),
        _const_spec((1, 2 * D_MODEL)),
        _const_spec((_T_END, D_MODEL)),
        pl.BlockSpec((tm, 2 * ROT_HALF), row),
        _const_spec((2 * ROT_HALF, 2 * LANES)),
        pl.BlockSpec((nsub, ROT_HALF, blk), blk3),
        pl.BlockSpec((nsub, ROT_HALF, blk), blk3),
    ]
    out_shape = [
        jax.ShapeDtypeStruct((T, GLA_QK), BF16),
        jax.ShapeDtypeStruct((T, GLA_QK), BF16),
        jax.ShapeDtypeStruct((T, GLA_V), BF16),
        jax.ShapeDtypeStruct((T, GLA_V), BF16),
        jax.ShapeDtypeStruct((T, GLA_QK), F32),
        jax.ShapeDtypeStruct((T, DIFF_QK), BF16),
        jax.ShapeDtypeStruct((T, 2 * D_MODEL), BF16),
        jax.ShapeDtypeStruct((nblk, DIFF_QK, blk), BF16),
        jax.ShapeDtypeStruct((nblk, DIFF_V, blk), BF16),
    ]
    out_specs = [
        pl.BlockSpec((tm, GLA_QK), row),
        pl.BlockSpec((tm, GLA_QK), row),
        pl.BlockSpec((tm, GLA_V), row),
        pl.BlockSpec((tm, GLA_V), row),
        pl.BlockSpec((tm, GLA_QK), row),
        pl.BlockSpec((tm, DIFF_QK), row),
        pl.BlockSpec((tm, 2 * D_MODEL), row),
        pl.BlockSpec((nsub, DIFF_QK, blk), blk3),
        pl.BlockSpec((nsub, DIFF_V, blk), blk3),
    ]
    return pl.pallas_call(
        _inproj_body,
        grid=(T // tm,),
        in_specs=in_specs,
        out_specs=out_specs,
        out_shape=out_shape,
        compiler_params=pltpu.CompilerParams(
            dimension_semantics=("parallel",), vmem_limit_bytes=VMEM_LIMIT),
        name="inproj",
    )(x2, pmw, wnat, wlow, wgk, bgk, bgate, wt, cs, rope_e, ct, st)


def _gla_body(q_ref, k_ref, v_ref, la_ref, og_ref, nw_ref, o_ref, state_ref):
    @pl.when(pl.program_id(1) == 0)
    def _():
        state_ref[...] = jnp.zeros_like(state_ref)

    n = GLA_SUPER
    ri = lax.broadcasted_iota(jnp.int32, (n, n), 0)
    ci = lax.broadcasted_iota(jnp.int32, (n, n), 1)
    tri = (ri >= ci).astype(BF16)
    same_chunk_causal = ((ri >= GLA_CHUNK) == (ci >= GLA_CHUNK)) & (ci <= ri)
    row = lax.broadcasted_iota(jnp.int32, (n, GLA_DK), 0)
    second = row >= GLA_CHUNK
    nw = nw_ref[...]

    for sc in range(TOK_BLK // n):
        r0 = sc * n
        la = la_ref[r0:r0 + n, :]
        hi = la.astype(BF16)
        lo = (la - hi.astype(F32)).astype(BF16)
        g_all = _dot(tri, hi) + _dot(tri, lo)

        pend = []
        for h in range(GLA_HEADS):
            ks = slice(h * GLA_DK, (h + 1) * GLA_DK)
            vs = slice(h * GLA_DV, (h + 1) * GLA_DV)
            g = g_all[:, ks]
            q = q_ref[r0:r0 + n, ks].astype(F32)
            k = k_ref[r0:r0 + n, ks].astype(F32)
            v = v_ref[r0:r0 + n, vs]
            g_mid = jnp.where(second, g[GLA_CHUNK + GLA_CHUNK // 2:GLA_CHUNK + GLA_CHUNK // 2 + 1],
                              g[GLA_CHUNK // 2:GLA_CHUNK // 2 + 1])
            g_c0 = g[GLA_CHUNK - 1:GLA_CHUNK]
            g_last = g[n - 1:n]

            qi = (q * jnp.exp(g - g_mid)).astype(BF16)
            kj = (k * jnp.exp(g_mid - g)).astype(BF16)
            a_diag = _dot_nt(qi, kj)
            qo = jnp.where(second, q * jnp.exp(jnp.minimum(g - g_c0, 0.0)), 0.0).astype(BF16)
            ko = jnp.where(second, 0.0, k * jnp.exp(jnp.minimum(g_c0 - g, 0.0))).astype(BF16)
            a = (jnp.where(same_chunk_causal, a_diag, 0.0) + _dot_nt(qo, ko)).astype(BF16)

            st = state_ref[h]
            o_inter = _dot((q * jnp.exp(g)).astype(BF16), st.astype(BF16))
            gt = g.T
            g_last_col = gt[:, n - 1:n]
            k_st_t = (k.T * jnp.exp(g_last_col - gt)).astype(BF16)
            state_ref[h] = st * jnp.exp(g_last_col) + _dot(k_st_t, v)
            pend.append((a, v, o_inter, vs))

        for a, v, o_inter, vs in pend:
            o = _dot(a, v) + o_inter
            og = og_ref[r0:r0 + n, vs].astype(F32)
            o_ref[r0:r0 + n, vs] = (_rms(o, nw) * og).astype(BF16)


def _gla(gq, gk, gv, la, og, nw, B, S):
    T = B * S
    tb = TOK_BLK
    nb = S // tb
    row = lambda b, i: (b * nb + i, 0)
    return pl.pallas_call(
        _gla_body,
        grid=(B, nb),
        in_specs=[
            pl.BlockSpec((tb, GLA_QK), row),
            pl.BlockSpec((tb, GLA_QK), row),
            pl.BlockSpec((tb, GLA_V), row),
            pl.BlockSpec((tb, GLA_QK), row),
            pl.BlockSpec((tb, GLA_V), row),
            pl.BlockSpec((1, GLA_DV), lambda b, i: (0, 0)),
        ],
        out_specs=pl.BlockSpec((tb, GLA_V), row),
        out_shape=jax.ShapeDtypeStruct((T, GLA_V), BF16),
        scratch_shapes=[pltpu.VMEM((GLA_HEADS, GLA_DK, GLA_DV), F32)],
        compiler_params=pltpu.CompilerParams(
            dimension_semantics=("parallel", "arbitrary"), vmem_limit_bytes=VMEM_LIMIT),
        name="gla",
    )(gq, gk, gv, la, og, nw)


def _attn_body(qt_ref, k_ref, vt_ref, lq1_ref, lk1_ref, lq2_ref, lk2_ref, nw_ref, o_ref,
               acc_ref, s_ref, *, lam_init):
    i = pl.program_id(2)
    blk = TOK_BLK
    chains = [(hh, sub) for hh in range(ATT_HPS) for sub in range(2)]
    nc = len(chains)

    qs = []
    for hh in range(ATT_HPS):
        qt = qt_ref[hh]
        frow = lax.broadcasted_iota(jnp.int32, qt.shape, 0)
        qs.append((jnp.where(frow < DIFF_DH, qt, jnp.zeros_like(qt)),
                   jnp.where(frow >= DIFF_DH, qt, jnp.zeros_like(qt))))
    ones_rows = jnp.ones((ATT_ONES_ROWS, blk), BF16)

    def issue_scores(jb, c):
        hh, sub = chains[c]
        kb = k_ref[pl.ds(pl.multiple_of(jb * blk, blk), blk), hh * 2 * DIFF_DH:(hh + 1) * 2 * DIFF_DH]
        s_ref[c] = _dot(kb, qs[hh][sub])

    def softmax_values(jb, c, m, masked):
        hh, sub = chains[c]
        s = s_ref[c]
        if masked:
            kr = lax.broadcasted_iota(jnp.int32, s.shape, 0)
            qc = lax.broadcasted_iota(jnp.int32, s.shape, 1)
            s = jnp.where(kr <= qc, s, NEG_BIG)
        m_new = jnp.maximum(m, jnp.max(s, axis=0, keepdims=True))
        p = jnp.exp2(s - m_new).astype(BF16)
        vx = jnp.concatenate([vt_ref[jb, hh], ones_rows], axis=0)
        acc_ref[c] = acc_ref[c] * jnp.exp2(m - m_new) + _dot(vx, p)
        return m_new

    acc_ref[...] = jnp.zeros_like(acc_ref)
    for c in range(nc):
        issue_scores(0, c)

    crow = lax.broadcasted_iota(jnp.int32, (nc, blk), 0)

    def body(j, m_all):
        for c in range(nc):
            m_new = softmax_values(j, c, m_all[c:c + 1], False)
            m_all = jnp.where(crow == c, m_new, m_all)
            issue_scores(j + 1, c)
        return m_all

    m_all = lax.fori_loop(0, i, body, jnp.full((nc, blk), NEG_BIG, F32))

    lam = (jnp.exp(jnp.sum(lq1_ref[...] * lk1_ref[...], axis=1, keepdims=True))
           - jnp.exp(jnp.sum(lq2_ref[...] * lk2_ref[...], axis=1, keepdims=True)) + lam_init)
    for hh in range(ATT_HPS):
        for c in (2 * hh, 2 * hh + 1):
            softmax_values(i, c, m_all[c:c + 1], True)
        a1, a2 = acc_ref[2 * hh], acc_ref[2 * hh + 1]
        o = (a1[:DIFF_DV] / a1[DIFF_DV:DIFF_DV + 1]
             - lam * (a2[:DIFF_DV] / a2[DIFF_DV:DIFF_DV + 1]))
        msq = jnp.mean(o * o, axis=0, keepdims=True)
        o = o * lax.rsqrt(msq + EPS) * nw_ref[...] * (1.0 - lam_init)
        o_ref[:, hh * DIFF_DV:(hh + 1) * DIFF_DV] = o.T.astype(BF16)


def _attn(dqt, dk, dvt, lq1, lk1, lq2, lk2, nw_col, B, S, lam_init):
    T = B * S
    blk = TOK_BLK
    nblk = S // blk
    H, hps = DIFF_HEADS, ATT_HPS
    qt5 = dqt.reshape(B, nblk, H, 2 * DIFF_DH, blk)
    vt5 = dvt.reshape(B, nblk, H, DIFF_DV, blk)
    k3 = dk.reshape(B, S, DIFF_QK)
    vec = pl.BlockSpec((1, DIFF_DH), lambda b, h, i: (0, 0))
    out = pl.pallas_call(
        functools.partial(_attn_body, lam_init=lam_init),
        grid=(B, H // hps, nblk),
        in_specs=[
            pl.BlockSpec((None, None, hps, 2 * DIFF_DH, blk), lambda b, h, i: (b, i, h, 0, 0)),
            pl.BlockSpec((None, S, hps * 2 * DIFF_DH), lambda b, h, i: (b, 0, h)),
            pl.BlockSpec((None, nblk, hps, DIFF_DV, blk), lambda b, h, i: (b, 0, h, 0, 0)),
            vec, vec, vec, vec,
            pl.BlockSpec((DIFF_DV, 1), lambda b, h, i: (0, 0)),
        ],
        out_specs=pl.BlockSpec((None, blk, hps * DIFF_DV), lambda b, h, i: (b, i, h)),
        out_shape=jax.ShapeDtypeStruct((B, S, DIFF_V), BF16),
        scratch_shapes=[pltpu.VMEM((hps * 2, DIFF_DV + ATT_ONES_ROWS, blk), F32),
                        pltpu.VMEM((hps * 2, blk, blk), F32)],
        compiler_params=pltpu.CompilerParams(
            dimension_semantics=("parallel", "parallel", "arbitrary"), vmem_limit_bytes=VMEM_LIMIT),
        name="attn",
    )(qt5, k3, vt5, lq1, lk1, lq2, lk2, nw_col)
    return out.reshape(T, DIFF_V)


def _merge_body(x_ref, oa_ref, ob_ref, ga_ref, gb_ref, wa_ref, wb_ref, wo_ref, pw_ref, o_ref):
    ya = _dot(oa_ref[...], wa_ref[...])
    yb = _dot(ob_ref[...], wb_ref[...])
    mix = ga_ref[...].astype(F32) * ya + gb_ref[...].astype(F32) * yb
    mixed = _dot(mix.astype(BF16), wo_ref[...])
    o_ref[...] = x_ref[...] + _rms(mixed, pw_ref[...])


def _merge(x2, oa, ob, gates, wa, wb, wo, pw):
    T = x2.shape[0]
    tm = MERGE_TM
    row = lambda i: (i, 0)
    return pl.pallas_call(
        _merge_body,
        grid=(T // tm,),
        in_specs=[
            pl.BlockSpec((tm, D_MODEL), row),
            pl.BlockSpec((tm, GLA_V), row),
            pl.BlockSpec((tm, DIFF_V), row),
            pl.BlockSpec((tm, D_MODEL), lambda i: (i, 0)),
            pl.BlockSpec((tm, D_MODEL), lambda i: (i, 1)),
            _const_spec((GLA_V, D_MODEL)),
            _const_spec((DIFF_V, D_MODEL)),
            _const_spec((D_MODEL, D_MODEL)),
            _const_spec((1, D_MODEL)),
        ],
        out_specs=pl.BlockSpec((tm, D_MODEL), row),
        out_shape=jax.ShapeDtypeStruct((T, D_MODEL), F32),
        compiler_params=pltpu.CompilerParams(
            dimension_semantics=("parallel",), vmem_limit_bytes=VMEM_LIMIT),
        name="merge",
    )(x2, oa, ob, gates, gates, wa, wb, wo, pw)


def _mlp_body(x_ref, wu_ref, wd_ref, prew_ref, postw_ref, o_ref):
    x = x_ref[...]
    h = _rms(x, prew_ref[...]).astype(BF16)
    f = None
    for c in range(D_FF // FF_CHUNK):
        a = jnp.maximum(_dot(h, wu_ref[:, c * FF_CHUNK:(c + 1) * FF_CHUNK]), 0.0)
        part = _dot((a * a).astype(BF16), wd_ref[c * FF_CHUNK:(c + 1) * FF_CHUNK, :])
        f = part if f is None else f + part
    o_ref[...] = x + _rms(f, postw_ref[...])


def _mlp(x1, wu, wd, prew, postw):
    T = x1.shape[0]
    tm = MLP_TM
    row = lambda i: (i, 0)
    return pl.pallas_call(
        _mlp_body,
        grid=(T // tm,),
        in_specs=[
            pl.BlockSpec((tm, D_MODEL), row),
            _const_spec((D_MODEL, D_FF)),
            _const_spec((D_FF, D_MODEL)),
            _const_spec((1, D_MODEL)),
            _const_spec((1, D_MODEL)),
        ],
        out_specs=pl.BlockSpec((tm, D_MODEL), row),
        out_shape=jax.ShapeDtypeStruct((T, D_MODEL), F32),
        compiler_params=pltpu.CompilerParams(
            dimension_semantics=("parallel",), vmem_limit_bytes=VMEM_LIMIT),
        name="mlp",
    )(x1, wu, wd, prew, postw)


def _rope_tables(positions):
    T = positions.size
    inv_freq = ROPE_THETA ** (-jnp.arange(0, ROT_DIM, 2, dtype=F32) / ROT_DIM)
    ang = positions.astype(F32).reshape(T, 1) * inv_freq
    cos, sin = jnp.cos(ang), jnp.sin(ang)
    cs = jnp.concatenate([cos, sin], axis=1)
    ct = cos.reshape(T // TOK_BLK, TOK_BLK, ROT_HALF).transpose(0, 2, 1)
    st = sin.reshape(T // TOK_BLK, TOK_BLK, ROT_HALF).transpose(0, 2, 1)
    return cs, ct, st


def _rope_expand_matrix():
    import numpy as np
    e = np.zeros((2 * ROT_HALF, 2 * LANES), np.float32)
    for base in range(0, LANES, DIFF_DH):
        for j in range(ROT_HALF):
            e[j, base + j] = 1.0
            e[j, base + ROT_HALF + j] = 1.0
            e[ROT_HALF + j, LANES + base + j] = -1.0
            e[ROT_HALF + j, LANES + base + ROT_HALF + j] = 1.0
    return jnp.asarray(e, BF16)


def kernel(x, positions, w_in, b_gate, w_gk_up, b_gk, gla_norm_w, lambda_q1, lambda_k1, lambda_q2,
           lambda_k2, diff_norm_w, w_branch_a, w_branch_b, w_out, pre_mix_w, post_mix_w, pre_mlp_w,
           post_mlp_w, w_up, w_down):
    B, S, D = x.shape
    T = B * S
    depth = w_in.shape[0]
    cs, ct, st = _rope_tables(positions)
    rope_e = _rope_expand_matrix()
    offs = [0]
    for n in IN_SIZES:
        offs.append(offs[-1] + n)
    (o_gq, o_gk, o_gv, o_og, o_low, o_dq, o_dk, o_dv, o_gates, o_end) = offs

    x2 = x.reshape(T, D)
    for l in range(depth):
        lam_init = 0.8 - 0.6 * math.exp(-0.3 * l)
        w = w_in[l]
        wnat = jnp.concatenate(
            [w[:, o_gq:o_gk], w[:, o_gk:o_gv], w[:, o_gv:o_og], w[:, o_og:o_low],
             w[:, o_dk:o_dv], w[:, o_gates:o_end]], axis=1).astype(BF16)
        wlow = jnp.pad(w[:, o_low:o_dq], ((0, 0), (0, MXU_COLS - GLA_RANK))).astype(BF16)
        wgk = jnp.pad(w_gk_up[l], ((0, MXU_COLS - GLA_RANK), (0, 0))).astype(BF16)
        wt = jnp.concatenate([w[:, o_dq:o_dk], w[:, o_dv:o_gates]], axis=1).T.astype(BF16)

        gq, gk, gv, og, la, dk, gates, dqt, dvt = _inproj(
            x2, pre_mix_w[l].reshape(1, D), wnat, wlow, wgk, b_gk[l].reshape(1, -1),
            b_gate[l].reshape(1, -1), wt, cs, rope_e, ct, st)

        o_a = _gla(gq, gk, gv, la, og, gla_norm_w[l].reshape(1, -1), B, S)
        o_b = _attn(dqt, dk, dvt,
                    lambda_q1[l].reshape(1, -1), lambda_k1[l].reshape(1, -1),
                    lambda_q2[l].reshape(1, -1), lambda_k2[l].reshape(1, -1),
                    diff_norm_w[l].reshape(-1, 1), B, S, lam_init)

        x1 = _merge(x2, o_a, o_b, gates, w_branch_a[l].astype(BF16), w_branch_b[l].astype(BF16),
                    w_out[l].astype(BF16), post_mix_w[l].reshape(1, D))
        x2 = _mlp(x1, w_up[l].astype(BF16), w_down[l].astype(BF16),
                  pre_mlp_w[l].reshape(1, D), post_mlp_w[l].reshape(1, D))
    return x2.reshape(B, S, D)
```

```python
import functools
import math

import jax
import jax.numpy as jnp
from jax import lax
from jax.experimental import pallas as pl
from jax.experimental.pallas import tpu as pltpu

F32 = jnp.float32
BF16 = jnp.bfloat16

D_MODEL = 1024
GLA_HEADS = 4
GLA_DK = 128
GLA_DV = 256
GLA_RANK = 16
GLA_GATE_NORM = 16.0
GLA_CHUNK = 64
DIFF_HEADS = 8
DIFF_DH = 64
DIFF_DV = 2 * DIFF_DH
ROT_DIM = DIFF_DH // 4
ROT_HALF = ROT_DIM // 2
ROPE_THETA = 500000.0
D_FF = 4 * D_MODEL
EPS = 1e-6

GLA_QK = GLA_HEADS * GLA_DK
GLA_V = GLA_HEADS * GLA_DV
DIFF_QK = DIFF_HEADS * 2 * DIFF_DH
DIFF_V = DIFF_HEADS * DIFF_DV
IN_SIZES = (GLA_QK, GLA_QK, GLA_V, GLA_V, GLA_RANK, DIFF_QK, DIFF_QK, DIFF_V, 2 * D_MODEL)

LANES = 128
MXU_COLS = 256
TOK_BLK = 256
GLA_SUPER = 2 * GLA_CHUNK
ATT_HPS = 8
ATT_KPT = 2
ATT_ONES_ROWS = 16
LOG2E = math.log2(math.e)
INPROJ_TM = 512
MERGE_TM = 512
MLP_TM = 512
FF_CHUNK = 1024
VMEM_LIMIT = 56 * 1024 * 1024
NEG_BIG = -1e30


def _rms(xf, w):
    ms = jnp.mean(xf * xf, axis=-1, keepdims=True)
    return xf * lax.rsqrt(ms + EPS) * w


def _dot(a, b):
    return jnp.dot(a, b, preferred_element_type=F32)


def _dot_nt(a, b):
    return lax.dot_general(a, b, (((1,), (1,)), ((), ())), preferred_element_type=F32)


def _const_spec(shape):
    nd = len(shape)
    return pl.BlockSpec(shape, lambda *_: (0,) * nd, pipeline_mode=pl.Buffered(1))


_N_GQ, _N_GK, _N_GV, _N_OG, _N_DK, _N_GATE = 0, 512, 1024, 2048, 3072, 4096
_N_END = 6144
_T_DQ, _T_DV, _T_END = 0, 1024, 2048


def _inproj_body(x_ref, pmw_ref, wnat_ref, wlow_ref, wgk_ref, bgk_ref, bgate_ref, wt_ref,
                 cs_ref, rope_e_ref, ct_ref, st_ref,
                 gq_ref, gk_ref, gv_ref, og_ref, la_ref, dk_ref, gates_ref,
                 dqt_ref, dvt_ref):
    u = _rms(x_ref[...], pmw_ref[...]).astype(BF16)

    def mm(lo, hi):
        return _dot(u, wnat_ref[:, lo:hi])

    gq_ref[...] = (mm(_N_GQ, _N_GK) * (GLA_DK ** -0.5)).astype(BF16)
    gk_ref[...] = mm(_N_GK, _N_GV).astype(BF16)
    gv_ref[...] = mm(_N_GV, _N_OG).astype(BF16)
    og = mm(_N_OG, _N_DK)
    og_ref[...] = (og * jax.nn.sigmoid(og)).astype(BF16)

    g_low = _dot(u, wlow_ref[...]).astype(BF16)
    z = _dot(g_low, wgk_ref[...]) + bgk_ref[...]
    la_ref[...] = (jnp.minimum(z, 0.0) - jnp.log(1.0 + jnp.exp(-jnp.abs(z)))) * (1.0 / GLA_GATE_NORM)

    dk = mm(_N_DK, _N_GATE)
    cs = cs_ref[...]
    cs_hi = cs.astype(BF16)
    cs_lo = (cs - cs_hi.astype(F32)).astype(BF16)
    tab = _dot(cs_hi, rope_e_ref[...]) + _dot(cs_lo, rope_e_ref[...])
    lane = lax.broadcasted_iota(jnp.int32, (1, LANES), 1) % DIFF_DH
    c = tab[:, :LANES] + jnp.where(lane >= ROT_DIM, 1.0, 0.0)
    sgn_sin = tab[:, LANES:]
    first_half = lane < ROT_HALF
    for h in range(DIFF_HEADS):
        t = dk[:, h * LANES:(h + 1) * LANES]
        partner = jnp.where(first_half, pltpu.roll(t, LANES - ROT_HALF, 1), pltpu.roll(t, ROT_HALF, 1))
        dk_ref[:, h * LANES:(h + 1) * LANES] = (t * c + partner * sgn_sin).astype(BF16)

    gates_ref[...] = jax.nn.sigmoid(mm(_N_GATE, _N_END) + bgate_ref[...]).astype(BF16)

    def mmt(lo, hi):
        return _dot_nt(wt_ref[lo:hi, :], u)

    nsub = dvt_ref.shape[0]
    dv = mmt(_T_DV, _T_END).astype(BF16)
    dq = mmt(_T_DQ, _T_DV) * (DIFF_DH ** -0.5 * LOG2E)
    for r in range(nsub):
        tok = slice(r * TOK_BLK, (r + 1) * TOK_BLK)
        dvt_ref[r] = dv[:, tok]
        ct, st = ct_ref[r], st_ref[r]
        pieces = []
        for g in range(DIFF_HEADS * 2):
            b = g * DIFF_DH
            t1, t2 = dq[b:b + ROT_HALF, tok], dq[b + ROT_HALF:b + ROT_DIM, tok]
            pieces += [t1 * ct - t2 * st, t2 * ct + t1 * st, dq[b + ROT_DIM:b + DIFF_DH, tok]]
        dqt_ref[r] = jnp.concatenate(pieces, axis=0).astype(BF16)


def _inproj(x2, pmw, wnat, wlow, wgk, bgk, bgate, wt, cs, rope_e, ct, st):
    T = x2.shape[0]
    tm = INPROJ_TM
    blk = TOK_BLK
    nsub = tm // blk
    nblk = T // blk
    row = lambda i: (i, 0)
    blk3 = lambda i: (i, 0, 0)
    in_specs = [
        pl.BlockSpec((tm, D_MODEL), row),
        _const_spec((1, D_MODEL)),
        _const_spec((D_MODEL, _N_END)),
        _const_spec((D_MODEL, MXU_COLS)),
        _const_spec((MXU_COLS, GLA_QK)),
        _const_spec((1, GLA_QK)),
        _const_spec((1, 2 * D_MODEL)),
        _const_spec((_T_END, D_MODEL)),
        pl.BlockSpec((tm, 2 * ROT_HALF), row),
        _const_spec((2 * ROT_HALF, 2 * LANES)),
        pl.BlockSpec((nsub, ROT_HALF, blk), blk3),
        pl.BlockSpec((nsub, ROT_HALF, blk), blk3),
    ]
    out_shape = [
        jax.ShapeDtypeStruct((T, GLA_QK), BF16),
        jax.ShapeDtypeStruct((T, GLA_QK), BF16),
        jax.ShapeDtypeStruct((T, GLA_V), BF16),
        jax.ShapeDtypeStruct((T, GLA_V), BF16),
        jax.ShapeDtypeStruct((T, GLA_QK), F32),
        jax.ShapeDtypeStruct((T, DIFF_QK), BF16),
        jax.ShapeDtypeStruct((T, 2 * D_MODEL), BF16),
        jax.ShapeDtypeStruct((nblk, DIFF_QK, blk), BF16),
        jax.ShapeDtypeStruct((nblk, DIFF_V, blk), BF16),
    ]
    out_specs = [
        pl.BlockSpec((tm, GLA_QK), row),
        pl.BlockSpec((tm, GLA_QK), row),
        pl.BlockSpec((tm, GLA_V), row),
        pl.BlockSpec((tm, GLA_V), row),
        pl.BlockSpec((tm, GLA_QK), row),
        pl.BlockSpec((tm, DIFF_QK), row),
        pl.BlockSpec((tm, 2 * D_MODEL), row),
        pl.BlockSpec((nsub, DIFF_QK, blk), blk3),
        pl.BlockSpec((nsub, DIFF_V, blk), blk3),
    ]
    return pl.pallas_call(
        _inproj_body,
        grid=(T // tm,),
        in_specs=in_specs,
        out_specs=out_specs,
        out_shape=out_shape,
        compiler_params=pltpu.CompilerParams(
            dimension_semantics=("parallel",), vmem_limit_bytes=VMEM_LIMIT),
        name="inproj",
    )(x2, pmw, wnat, wlow, wgk, bgk, bgate, wt, cs, rope_e, ct, st)


def _gla_body(q_ref, k_ref, v_ref, la_ref, og_ref, nw_ref, o_ref, state_ref):
    @pl.when(pl.program_id(1) == 0)
    def _():
        state_ref[...] = jnp.zeros_like(state_ref)

    n = GLA_SUPER
    ri = lax.broadcasted_iota(jnp.int32, (n, n), 0)
    ci = lax.broadcasted_iota(jnp.int32, (n, n), 1)
    tri = (ri >= ci).astype(BF16)
    same_chunk_causal = ((ri >= GLA_CHUNK) == (ci >= GLA_CHUNK)) & (ci <= ri)
    row = lax.broadcasted_iota(jnp.int32, (n, GLA_DK), 0)
    second = row >= GLA_CHUNK
    nw = nw_ref[...]

    for sc in range(TOK_BLK // n):
        r0 = sc * n
        la = la_ref[r0:r0 + n, :]
        hi = la.astype(BF16)
        lo = (la - hi.astype(F32)).astype(BF16)
        g_all = _dot(tri, hi) + _dot(tri, lo)

        pend = []
        for h in range(GLA_HEADS):
            ks = slice(h * GLA_DK, (h + 1) * GLA_DK)
            vs = slice(h * GLA_DV, (h + 1) * GLA_DV)
            g = g_all[:, ks]
            q = q_ref[r0:r0 + n, ks].astype(F32)
            k = k_ref[r0:r0 + n, ks].astype(F32)
            v = v_ref[r0:r0 + n, vs]
            g_mid = jnp.where(second, g[GLA_CHUNK + GLA_CHUNK // 2:GLA_CHUNK + GLA_CHUNK // 2 + 1],
                              g[GLA_CHUNK // 2:GLA_CHUNK // 2 + 1])
            g_c0 = g[GLA_CHUNK - 1:GLA_CHUNK]
            g_last = g[n - 1:n]

            qi = (q * jnp.exp(g - g_mid)).astype(BF16)
            kj = (k * jnp.exp(g_mid - g)).astype(BF16)
            a_diag = _dot_nt(qi, kj)
            qo = jnp.where(second, q * jnp.exp(jnp.minimum(g - g_c0, 0.0)), 0.0).astype(BF16)
            ko = jnp.where(second, 0.0, k * jnp.exp(jnp.minimum(g_c0 - g, 0.0))).astype(BF16)
            a = (jnp.where(same_chunk_causal, a_diag, 0.0) + _dot_nt(qo, ko)).astype(BF16)

            st = state_ref[h]
            o_inter = _dot((q * jnp.exp(g)).astype(BF16), st.astype(BF16))
            gt = g.T
            g_last_col = gt[:, n - 1:n]
            k_st_t = (k.T * jnp.exp(g_last_col - gt)).astype(BF16)
            state_ref[h] = st * jnp.exp(g_last_col) + _dot(k_st_t, v)
            pend.append((a, v, o_inter, vs))

        for a, v, o_inter, vs in pend:
            o = _dot(a, v) + o_inter
            og = og_ref[r0:r0 + n, vs].astype(F32)
            o_ref[r0:r0 + n, vs] = (_rms(o, nw) * og).astype(BF16)


def _gla(gq, gk, gv, la, og, nw, B, S):
    T = B * S
    tb = TOK_BLK
    nb = S // tb
    row = lambda b, i: (b * nb + i, 0)
    return pl.pallas_call(
        _gla_body,
        grid=(B, nb),
        in_specs=[
            pl.BlockSpec((tb, GLA_QK), row),
            pl.BlockSpec((tb, GLA_QK), row),
            pl.BlockSpec((tb, GLA_V), row),
            pl.BlockSpec((tb, GLA_QK), row),
            pl.BlockSpec((tb, GLA_V), row),
            pl.BlockSpec((1, GLA_DV), lambda b, i: (0, 0)),
        ],
        out_specs=pl.BlockSpec((tb, GLA_V), row),
        out_shape=jax.ShapeDtypeStruct((T, GLA_V), BF16),
        scratch_shapes=[pltpu.VMEM((GLA_HEADS, GLA_DK, GLA_DV), F32)],
        compiler_params=pltpu.CompilerParams(
            dimension_semantics=("parallel", "arbitrary"), vmem_limit_bytes=VMEM_LIMIT),
        name="gla",
    )(gq, gk, gv, la, og, nw)


def _attn_body(qt_ref, k_ref, vt_ref, lq1_ref, lk1_ref, lq2_ref, lk2_ref, nw_ref, o_ref,
               acc_ref, s_ref, *, lam_init):
    i = pl.program_id(2)
    blk = TOK_BLK
    chains = [(hh, sub) for hh in range(ATT_HPS) for sub in range(2)]
    nc = len(chains)

    qs = []
    for hh in range(ATT_HPS):
        qt = qt_ref[hh]
        frow = lax.broadcasted_iota(jnp.int32, qt.shape, 0)
        qs.append((jnp.where(frow < DIFF_DH, qt, jnp.zeros_like(qt)),
                   jnp.where(frow >= DIFF_DH, qt, jnp.zeros_like(qt))))

    kpt = ATT_KPT

    def issue_scores(jp, c):
        hh, sub = chains[c]
        kb = k_ref[pl.ds(pl.multiple_of(jp * (kpt * blk), kpt * blk), kpt * blk),
                   hh * 2 * DIFF_DH:(hh + 1) * 2 * DIFF_DH]
        s_ref[c] = _dot(kb, qs[hh][sub])

    def softmax_values(jb, nb, c, m, keep):
        hh, sub = chains[c]
        s = s_ref[c, :nb * blk]
        if keep is not None:
            s = jnp.where(keep, s, NEG_BIG)
        m_new = jnp.maximum(m, jnp.max(s, axis=0, keepdims=True))
        p = jnp.exp2(s - m_new).astype(BF16)
        vt = jnp.concatenate([vt_ref[jb + r, hh] for r in range(nb)], axis=1)
        vx = jnp.concatenate([vt, jnp.ones((ATT_ONES_ROWS, nb * blk), BF16)], axis=0)
        acc_ref[c] = acc_ref[c] * jnp.exp2(m - m_new) + _dot(vx, p)
        return m_new

    acc_ref[...] = jnp.zeros_like(acc_ref)
    for c in range(nc):
        issue_scores(0, c)

    crow = lax.broadcasted_iota(jnp.int32, (nc, blk), 0)

    def body(t, m_all):
        for c in range(nc):
            m_new = softmax_values(kpt * t, kpt, c, m_all[c:c + 1], None)
            m_all = jnp.where(crow == c, m_new, m_all)
            issue_scores(t + 1, c)
        return m_all

    npair = i // kpt
    m_all = lax.fori_loop(0, npair, body, jnp.full((nc, blk), NEG_BIG, F32))

    lam = (jnp.exp(jnp.sum(lq1_ref[...] * lk1_ref[...], axis=1, keepdims=True))
           - jnp.exp(jnp.sum(lq2_ref[...] * lk2_ref[...], axis=1, keepdims=True)) + lam_init)

    def finish(nb, keep):
        for hh in range(ATT_HPS):
            for c in (2 * hh, 2 * hh + 1):
                softmax_values(kpt * npair, nb, c, m_all[c:c + 1], keep)
            a1, a2 = acc_ref[2 * hh], acc_ref[2 * hh + 1]
            o = (a1[:DIFF_DV] / a1[DIFF_DV:DIFF_DV + 1]
                 - lam * (a2[:DIFF_DV] / a2[DIFF_DV:DIFF_DV + 1]))
            msq = jnp.mean(o * o, axis=0, keepdims=True)
            o = o * lax.rsqrt(msq + EPS) * nw_ref[...] * (1.0 - lam_init)
            o_ref[:, hh * DIFF_DV:(hh + 1) * DIFF_DV] = o.T.astype(BF16)

    kr = lax.broadcasted_iota(jnp.int32, (kpt * blk, blk), 0)
    qc = lax.broadcasted_iota(jnp.int32, (kpt * blk, blk), 1)

    @pl.when(i % kpt == 0)
    def _():
        finish(1, (kr <= qc)[:blk])

    @pl.when(i % kpt == 1)
    def _():
        finish(2, (kr < blk) | (kr - blk <= qc))


def _attn(dqt, dk, dvt, lq1, lk1, lq2, lk2, nw_col, B, S, lam_init):
    T = B * S
    blk = TOK_BLK
    nblk = S // blk
    H, hps = DIFF_HEADS, ATT_HPS
    qt5 = dqt.reshape(B, nblk, H, 2 * DIFF_DH, blk)
    vt5 = dvt.reshape(B, nblk, H, DIFF_DV, blk)
    k3 = dk.reshape(B, S, DIFF_QK)
    vec = pl.BlockSpec((1, DIFF_DH), lambda b, h, i: (0, 0))
    out = pl.pallas_call(
        functools.partial(_attn_body, lam_init=lam_init),
        grid=(B, H // hps, nblk),
        in_specs=[
            pl.BlockSpec((None, None, hps, 2 * DIFF_DH, blk), lambda b, h, i: (b, i, h, 0, 0)),
            pl.BlockSpec((None, S, hps * 2 * DIFF_DH), lambda b, h, i: (b, 0, h)),
            pl.BlockSpec((None, nblk, hps, DIFF_DV, blk), lambda b, h, i: (b, 0, h, 0, 0)),
            vec, vec, vec, vec,
            pl.BlockSpec((DIFF_DV, 1), lambda b, h, i: (0, 0)),
        ],
        out_specs=pl.BlockSpec((None, blk, hps * DIFF_DV), lambda b, h, i: (b, i, h)),
        out_shape=jax.ShapeDtypeStruct((B, S, DIFF_V), BF16),
        scratch_shapes=[pltpu.VMEM((hps * 2, DIFF_DV + ATT_ONES_ROWS, blk), F32),
                        pltpu.VMEM((hps * 2, ATT_KPT * blk, blk), F32)],
        compiler_params=pltpu.CompilerParams(
            dimension_semantics=("parallel", "parallel", "arbitrary"), vmem_limit_bytes=VMEM_LIMIT),
        name="attn",
    )(qt5, k3, vt5, lq1, lk1, lq2, lk2, nw_col)
    return out.reshape(T, DIFF_V)


def _merge_body(x_ref, oa_ref, ob_ref, ga_ref, gb_ref, wa_ref, wb_ref, wo_ref, pw_ref, o_ref):
    ya = _dot(oa_ref[...], wa_ref[...])
    yb = _dot(ob_ref[...], wb_ref[...])
    mix = ga_ref[...].astype(F32) * ya + gb_ref[...].astype(F32) * yb
    mixed = _dot(mix.astype(BF16), wo_ref[...])
    o_ref[...] = x_ref[...] + _rms(mixed, pw_ref[...])


def _merge(x2, oa, ob, gates, wa, wb, wo, pw):
    T = x2.shape[0]
    tm = MERGE_TM
    row = lambda i: (i, 0)
    return pl.pallas_call(
        _merge_body,
        grid=(T // tm,),
        in_specs=[
            pl.BlockSpec((tm, D_MODEL), row),
            pl.BlockSpec((tm, GLA_V), row),
            pl.BlockSpec((tm, DIFF_V), row),
            pl.BlockSpec((tm, D_MODEL), lambda i: (i, 0)),
            pl.BlockSpec((tm, D_MODEL), lambda i: (i, 1)),
            _const_spec((GLA_V, D_MODEL)),
            _const_spec((DIFF_V, D_MODEL)),
            _const_spec((D_MODEL, D_MODEL)),
            _const_spec((1, D_MODEL)),
        ],
        out_specs=pl.BlockSpec((tm, D_MODEL), row),
        out_shape=jax.ShapeDtypeStruct((T, D_MODEL), F32),
        compiler_params=pltpu.CompilerParams(
            dimension_semantics=("parallel",), vmem_limit_bytes=VMEM_LIMIT),
        name="merge",
    )(x2, oa, ob, gates, gates, wa, wb, wo, pw)


def _mlp_body(x_ref, wu_ref, wd_ref, prew_ref, postw_ref, o_ref):
    x = x_ref[...]
    h = _rms(x, prew_ref[...]).astype(BF16)
    f = None
    for c in range(D_FF // FF_CHUNK):
        a = jnp.maximum(_dot(h, wu_ref[:, c * FF_CHUNK:(c + 1) * FF_CHUNK]), 0.0)
        part = _dot((a * a).astype(BF16), wd_ref[c * FF_CHUNK:(c + 1) * FF_CHUNK, :])
        f = part if f is None else f + part
    o_ref[...] = x + _rms(f, postw_ref[...])


def _mlp(x1, wu, wd, prew, postw):
    T = x1.shape[0]
    tm = MLP_TM
    row = lambda i: (i, 0)
    return pl.pallas_call(
        _mlp_body,
        grid=(T // tm,),
        in_specs=[
            pl.BlockSpec((tm, D_MODEL), row),
            _const_spec((D_MODEL, D_FF)),
            _const_spec((D_FF, D_MODEL)),
            _const_spec((1, D_MODEL)),
            _const_spec((1, D_MODEL)),
        ],
        out_specs=pl.BlockSpec((tm, D_MODEL), row),
        out_shape=jax.ShapeDtypeStruct((T, D_MODEL), F32),
        compiler_params=pltpu.CompilerParams(
            dimension_semantics=("parallel",), vmem_limit_bytes=VMEM_LIMIT),
        name="mlp",
    )(x1, wu, wd, prew, postw)


def _rope_tables(positions):
    T = positions.size
    inv_freq = ROPE_THETA ** (-jnp.arange(0, ROT_DIM, 2, dtype=F32) / ROT_DIM)
    ang = positions.astype(F32).reshape(T, 1) * inv_freq
    cos, sin = jnp.cos(ang), jnp.sin(ang)
    cs = jnp.concatenate([cos, sin], axis=1)
    ct = cos.reshape(T // TOK_BLK, TOK_BLK, ROT_HALF).transpose(0, 2, 1)
    st = sin.reshape(T // TOK_BLK, TOK_BLK, ROT_HALF).transpose(0, 2, 1)
    return cs, ct, st


def _rope_expand_matrix():
    import numpy as np
    e = np.zeros((2 * ROT_HALF, 2 * LANES), np.float32)
    for base in range(0, LANES, DIFF_DH):
        for j in range(ROT_HALF):
            e[j, base + j] = 1.0
            e[j, base + ROT_HALF + j] = 1.0
            e[ROT_HALF + j, LANES + base + j] = -1.0
            e[ROT_HALF + j, LANES + base + ROT_HALF + j] = 1.0
    return jnp.asarray(e, BF16)


def kernel(x, positions, w_in, b_gate, w_gk_up, b_gk, gla_norm_w, lambda_q1, lambda_k1, lambda_q2,
           lambda_k2, diff_norm_w, w_branch_a, w_branch_b, w_out, pre_mix_w, post_mix_w, pre_mlp_w,
           post_mlp_w, w_up, w_down):
    B, S, D = x.shape
    T = B * S
    depth = w_in.shape[0]
    cs, ct, st = _rope_tables(positions)
    rope_e = _rope_expand_matrix()
    offs = [0]
    for n in IN_SIZES:
        offs.append(offs[-1] + n)
    (o_gq, o_gk, o_gv, o_og, o_low, o_dq, o_dk, o_dv, o_gates, o_end) = offs

    x2 = x.reshape(T, D)
    for l in range(depth):
        lam_init = 0.8 - 0.6 * math.exp(-0.3 * l)
        w = w_in[l]
        wnat = jnp.concatenate(
            [w[:, o_gq:o_gk], w[:, o_gk:o_gv], w[:, o_gv:o_og], w[:, o_og:o_low],
             w[:, o_dk:o_dv], w[:, o_gates:o_end]], axis=1).astype(BF16)
        wlow = jnp.pad(w[:, o_low:o_dq], ((0, 0), (0, MXU_COLS - GLA_RANK))).astype(BF16)
        wgk = jnp.pad(w_gk_up[l], ((0, MXU_COLS - GLA_RANK), (0, 0))).astype(BF16)
        wt = jnp.concatenate([w[:, o_dq:o_dk], w[:, o_dv:o_gates]], axis=1).T.astype(BF16)

        gq, gk, gv, og, la, dk, gates, dqt, dvt = _inproj(
            x2, pre_mix_w[l].reshape(1, D), wnat, wlow, wgk, b_gk[l].reshape(1, -1),
            b_gate[l].reshape(1, -1), wt, cs, rope_e, ct, st)

        o_a = _gla(gq, gk, gv, la, og, gla_norm_w[l].reshape(1, -1), B, S)
        o_b = _attn(dqt, dk, dvt,
                    lambda_q1[l].reshape(1, -1), lambda_k1[l].reshape(1, -1),
                    lambda_q2[l].reshape(1, -1), lambda_k2[l].reshape(1, -1),
                    diff_norm_w[l].reshape(-1, 1), B, S, lam_init)

        x1 = _merge(x2, o_a, o_b, gates, w_branch_a[l].astype(BF16), w_branch_b[l].astype(BF16),
                    w_out[l].astype(BF16), post_mix_w[l].reshape(1, D))
        x2 = _mlp(x1, w_up[l].astype(BF16), w_down[l].astype(BF16),
                  pre_mlp_w[l].reshape(1, D), post_mlp_w[l].reshape(1, D))
    return x2.reshape(B, S, D)
```

```python
import functools
import math

import jax
import jax.numpy as jnp
from jax import lax
from jax.experimental import pallas as pl
from jax.experimental.pallas import tpu as pltpu

F32 = jnp.float32
BF16 = jnp.bfloat16

D_MODEL = 1024
GLA_HEADS = 4
GLA_DK = 128
GLA_DV = 256
GLA_RANK = 16
GLA_GATE_NORM = 16.0
GLA_CHUNK = 64
DIFF_HEADS = 8
DIFF_DH = 64
DIFF_DV = 2 * DIFF_DH
ROT_DIM = DIFF_DH // 4
ROT_HALF = ROT_DIM // 2
ROPE_THETA = 500000.0
D_FF = 4 * D_MODEL
EPS = 1e-6

GLA_QK = GLA_HEADS * GLA_DK
GLA_V = GLA_HEADS * GLA_DV
DIFF_QK = DIFF_HEADS * 2 * DIFF_DH
DIFF_V = DIFF_HEADS * DIFF_DV
IN_SIZES = (GLA_QK, GLA_QK, GLA_V, GLA_V, GLA_RANK, DIFF_QK, DIFF_QK, DIFF_V, 2 * D_MODEL)

LANES = 128
MXU_COLS = 256
TOK_BLK = 256
GLA_SUPER = 2 * GLA_CHUNK
GLA_BPS = 4
ATT_HPS = 8
ATT_KPT = 2
ATT_ONES_ROWS = 16
LOG2E = math.log2(math.e)
INPROJ_TM = 512
MERGE_TM = 512
MLP_TM = 512
FF_CHUNK = 1024
VMEM_LIMIT = 56 * 1024 * 1024
NEG_BIG = -1e30


def _rms(xf, w):
    ms = jnp.mean(xf * xf, axis=-1, keepdims=True)
    return xf * lax.rsqrt(ms + EPS) * w


def _dot(a, b):
    return jnp.dot(a, b, preferred_element_type=F32)


def _dot_nt(a, b):
    return lax.dot_general(a, b, (((1,), (1,)), ((), ())), preferred_element_type=F32)


def _const_spec(shape):
    nd = len(shape)
    return pl.BlockSpec(shape, lambda *_: (0,) * nd, pipeline_mode=pl.Buffered(1))


_N_GQ, _N_GK, _N_GV, _N_OG, _N_DK, _N_GATE = 0, 512, 1024, 2048, 3072, 4096
_N_END = 6144
_T_DQ, _T_DV, _T_END = 0, 1024, 2048


def _inproj_body(x_ref, pmw_ref, wnat_ref, wlow_ref, wgk_ref, bgk_ref, bgate_ref, wt_ref,
                 cs_ref, rope_e_ref, ct_ref, st_ref,
                 gq_ref, gk_ref, gv_ref, og_ref, la_ref, dk_ref, gates_ref,
                 dqt_ref, dvt_ref):
    u = _rms(x_ref[...], pmw_ref[...]).astype(BF16)

    def mm(lo, hi):
        return _dot(u, wnat_ref[:, lo:hi])

    og = mm(_N_OG, _N_DK)
    og_ref[...] = (og * jax.nn.sigmoid(og)).astype(BF16)

    g_low = _dot(u, wlow_ref[...]).astype(BF16)
    z = _dot(g_low, wgk_ref[...]) + bgk_ref[...]
    la_ref[...] = (jnp.minimum(z, 0.0) - jnp.log(1.0 + jnp.exp(-jnp.abs(z)))) * (1.0 / GLA_GATE_NORM)

    dk = mm(_N_DK, _N_GATE)
    cs = cs_ref[...]
    cs_hi = cs.astype(BF16)
    cs_lo = (cs - cs_hi.astype(F32)).astype(BF16)
    tab = _dot(cs_hi, rope_e_ref[...]) + _dot(cs_lo, rope_e_ref[...])
    lane = lax.broadcasted_iota(jnp.int32, (1, LANES), 1) % DIFF_DH
    c = tab[:, :LANES] + jnp.where(lane >= ROT_DIM, 1.0, 0.0)
    sgn_sin = tab[:, LANES:]
    first_half = lane < ROT_HALF
    for h in range(DIFF_HEADS):
        t = dk[:, h * LANES:(h + 1) * LANES]
        partner = jnp.where(first_half, pltpu.roll(t, LANES - ROT_HALF, 1), pltpu.roll(t, ROT_HALF, 1))
        dk_ref[:, h * LANES:(h + 1) * LANES] = (t * c + partner * sgn_sin).astype(BF16)

    gates_ref[...] = jax.nn.sigmoid(mm(_N_GATE, _N_END) + bgate_ref[...]).astype(BF16)

    def mmt(lo, hi):
        return _dot_nt(wt_ref[lo:hi, :], u)

    nsub = dvt_ref.shape[0]
    dv = mmt(_T_DV, _T_END).astype(BF16)
    dq = mmt(_T_DQ, _T_DV) * (DIFF_DH ** -0.5 * LOG2E)
    for r in range(nsub):
        tok = slice(r * TOK_BLK, (r + 1) * TOK_BLK)
        dvt_ref[r] = dv[:, tok]
        ct, st = ct_ref[r], st_ref[r]
        pieces = []
        for g in range(DIFF_HEADS * 2):
            b = g * DIFF_DH
            t1, t2 = dq[b:b + ROT_HALF, tok], dq[b + ROT_HALF:b + ROT_DIM, tok]
            pieces += [t1 * ct - t2 * st, t2 * ct + t1 * st, dq[b + ROT_DIM:b + DIFF_DH, tok]]
        dqt_ref[r] = jnp.concatenate(pieces, axis=0).astype(BF16)

    gq_ref[...] = (mm(_N_GQ, _N_GK) * (GLA_DK ** -0.5)).astype(BF16)
    gk_ref[...] = mm(_N_GK, _N_GV).astype(BF16)
    gv_ref[...] = mm(_N_GV, _N_OG).astype(BF16)


def _inproj(x2, pmw, wnat, wlow, wgk, bgk, bgate, wt, cs, rope_e, ct, st):
    T = x2.shape[0]
    tm = INPROJ_TM
    blk = TOK_BLK
    nsub = tm // blk
    nblk = T // blk
    row = lambda i: (i, 0)
    blk3 = lambda i: (i, 0, 0)
    in_specs = [
        pl.BlockSpec((tm, D_MODEL), row),
        _const_spec((1, D_MODEL)),
        _const_spec((D_MODEL, _N_END)),
        _const_spec((D_MODEL, MXU_COLS)),
        _const_spec((MXU_COLS, GLA_QK)),
        _const_spec((1, GLA_QK)),
        _const_spec((1, 2 * D_MODEL)),
        _const_spec((_T_END, D_MODEL)),
        pl.BlockSpec((tm, 2 * ROT_HALF), row),
        _const_spec((2 * ROT_HALF, 2 * LANES)),
        pl.BlockSpec((nsub, ROT_HALF, blk), blk3),
        pl.BlockSpec((nsub, ROT_HALF, blk), blk3),
    ]
    out_shape = [
        jax.ShapeDtypeStruct((T, GLA_QK), BF16),
        jax.ShapeDtypeStruct((T, GLA_QK), BF16),
        jax.ShapeDtypeStruct((T, GLA_V), BF16),
        jax.ShapeDtypeStruct((T, GLA_V), BF16),
        jax.ShapeDtypeStruct((T, GLA_QK), F32),
        jax.ShapeDtypeStruct((T, DIFF_QK), BF16),
        jax.ShapeDtypeStruct((T, 2 * D_MODEL), BF16),
        jax.ShapeDtypeStruct((nblk, DIFF_QK, blk), BF16),
        jax.ShapeDtypeStruct((nblk, DIFF_V, blk), BF16),
    ]
    out_specs = [
        pl.BlockSpec((tm, GLA_QK), row),
        pl.BlockSpec((tm, GLA_QK), row),
        pl.BlockSpec((tm, GLA_V), row),
        pl.BlockSpec((tm, GLA_V), row),
        pl.BlockSpec((tm, GLA_QK), row),
        pl.BlockSpec((tm, DIFF_QK), row),
        pl.BlockSpec((tm, 2 * D_MODEL), row),
        pl.BlockSpec((nsub, DIFF_QK, blk), blk3),
        pl.BlockSpec((nsub, DIFF_V, blk), blk3),
    ]
    return pl.pallas_call(
        _inproj_body,
        grid=(T // tm,),
        in_specs=in_specs,
        out_specs=out_specs,
        out_shape=out_shape,
        compiler_params=pltpu.CompilerParams(
            dimension_semantics=("parallel",), vmem_limit_bytes=VMEM_LIMIT),
        name="inproj",
    )(x2, pmw, wnat, wlow, wgk, bgk, bgate, wt, cs, rope_e, ct, st)


def _gla_body(q_ref, k_ref, v_ref, la_ref, og_ref, nw_ref, o_ref, state_ref):
    @pl.when(pl.program_id(1) == 0)
    def _():
        state_ref[...] = jnp.zeros_like(state_ref)

    n = GLA_SUPER
    ri = lax.broadcasted_iota(jnp.int32, (n, n), 0)
    ci = lax.broadcasted_iota(jnp.int32, (n, n), 1)
    tri = (ri >= ci).astype(BF16)
    same_chunk_causal = ((ri >= GLA_CHUNK) == (ci >= GLA_CHUNK)) & (ci <= ri)
    row = lax.broadcasted_iota(jnp.int32, (n, GLA_DK), 0)
    second = row >= GLA_CHUNK
    nw = nw_ref[...]

    for sc, bb in [(sc, bb) for sc in range(TOK_BLK // n) for bb in range(GLA_BPS)]:
        r0 = sc * n
        la = la_ref[bb, r0:r0 + n, :]
        hi = la.astype(BF16)
        lo = (la - hi.astype(F32)).astype(BF16)
        g_all = _dot(tri, hi) + _dot(tri, lo)

        pend = []
        for h in range(GLA_HEADS):
            ks = slice(h * GLA_DK, (h + 1) * GLA_DK)
            vs = slice(h * GLA_DV, (h + 1) * GLA_DV)
            g = g_all[:, ks]
            q = q_ref[bb, r0:r0 + n, ks].astype(F32)
            k = k_ref[bb, r0:r0 + n, ks].astype(F32)
            v = v_ref[bb, r0:r0 + n, vs]
            g_mid = jnp.where(second, g[GLA_CHUNK + GLA_CHUNK // 2:GLA_CHUNK + GLA_CHUNK // 2 + 1],
                              g[GLA_CHUNK // 2:GLA_CHUNK // 2 + 1])
            g_c0 = g[GLA_CHUNK - 1:GLA_CHUNK]
            g_last = g[n - 1:n]

            qi = (q * jnp.exp(g - g_mid)).astype(BF16)
            kj = (k * jnp.exp(g_mid - g)).astype(BF16)
            a_diag = _dot_nt(qi, kj)
            qo = jnp.where(second, q * jnp.exp(jnp.minimum(g - g_c0, 0.0)), 0.0).astype(BF16)
            ko = jnp.where(second, 0.0, k * jnp.exp(jnp.minimum(g_c0 - g, 0.0))).astype(BF16)
            a = (jnp.where(same_chunk_causal, a_diag, 0.0) + _dot_nt(qo, ko)).astype(BF16)

            st = state_ref[bb, h]
            o_inter = _dot((q * jnp.exp(g)).astype(BF16), st.astype(BF16))
            gt = g.T
            g_last_col = gt[:, n - 1:n]
            k_st_t = (k.T * jnp.exp(g_last_col - gt)).astype(BF16)
            state_ref[bb, h] = st * jnp.exp(g_last_col) + _dot(k_st_t, v)
            pend.append((a, v, o_inter, vs))

        for a, v, o_inter, vs in pend:
            o = _dot(a, v) + o_inter
            og = og_ref[bb, r0:r0 + n, vs].astype(F32)
            o_ref[bb, r0:r0 + n, vs] = (_rms(o, nw) * og).astype(BF16)


def _gla(gq, gk, gv, la, og, nw, B, S):
    T = B * S
    tb = TOK_BLK
    bps = GLA_BPS
    blkmap = lambda b, i: (b, i, 0)
    r3 = lambda a: a.reshape(B, S, a.shape[-1])
    out = pl.pallas_call(
        _gla_body,
        grid=(B // bps, S // tb),
        in_specs=[
            pl.BlockSpec((bps, tb, GLA_QK), blkmap),
            pl.BlockSpec((bps, tb, GLA_QK), blkmap),
            pl.BlockSpec((bps, tb, GLA_V), blkmap),
            pl.BlockSpec((bps, tb, GLA_QK), blkmap),
            pl.BlockSpec((bps, tb, GLA_V), blkmap),
            pl.BlockSpec((1, GLA_DV), lambda b, i: (0, 0)),
        ],
        out_specs=pl.BlockSpec((bps, tb, GLA_V), blkmap),
        out_shape=jax.ShapeDtypeStruct((B, S, GLA_V), BF16),
        scratch_shapes=[pltpu.VMEM((bps, GLA_HEADS, GLA_DK, GLA_DV), F32)],
        compiler_params=pltpu.CompilerParams(
            dimension_semantics=("parallel", "arbitrary"), vmem_limit_bytes=VMEM_LIMIT),
        name="gla",
    )(r3(gq), r3(gk), r3(gv), r3(la), r3(og), nw)
    return out.reshape(T, GLA_V)


def _attn_body(qt_ref, qtn_ref, k_ref, vt_ref, lq1_ref, lk1_ref, lq2_ref, lk2_ref, nw_ref, o_ref,
               acc_ref, s_ref, *, lam_init):
    i = pl.program_id(2)
    blk = TOK_BLK
    chains = [(hh, sub) for hh in range(ATT_HPS) for sub in range(2)]
    nc = len(chains)

    def split_subheads(ref):
        out = []
        for hh in range(ATT_HPS):
            qt = ref[hh]
            frow = lax.broadcasted_iota(jnp.int32, qt.shape, 0)
            out.append((jnp.where(frow < DIFF_DH, qt, jnp.zeros_like(qt)),
                        jnp.where(frow >= DIFF_DH, qt, jnp.zeros_like(qt))))
        return out

    qs = split_subheads(qt_ref)
    qs_next = split_subheads(qtn_ref)

    kpt = ATT_KPT

    def issue_scores(jp, c, q=None):
        hh, sub = chains[c]
        kb = k_ref[pl.ds(pl.multiple_of(jp * (kpt * blk), kpt * blk), kpt * blk),
                   hh * 2 * DIFF_DH:(hh + 1) * 2 * DIFF_DH]
        s_ref[c] = _dot(kb, (qs if q is None else q)[hh][sub])

    def softmax_values(jb, nb, c, m, keep):
        hh, sub = chains[c]
        s = s_ref[c, :nb * blk]
        if keep is not None:
            s = jnp.where(keep, s, NEG_BIG)
        m_new = jnp.maximum(m, jnp.max(s, axis=0, keepdims=True))
        p = jnp.exp2(s - m_new).astype(BF16)
        vt = jnp.concatenate([vt_ref[jb + r, hh] for r in range(nb)], axis=1)
        vx = jnp.concatenate([vt, jnp.ones((ATT_ONES_ROWS, nb * blk), BF16)], axis=0)
        acc_ref[c] = acc_ref[c] * jnp.exp2(m - m_new) + _dot(vx, p)
        return m_new

    acc_ref[...] = jnp.zeros_like(acc_ref)

    @pl.when(i == 0)
    def _():
        for c in range(nc):
            issue_scores(0, c)

    crow = lax.broadcasted_iota(jnp.int32, (nc, blk), 0)

    def body(t, m_all):
        for c in range(nc):
            m_new = softmax_values(kpt * t, kpt, c, m_all[c:c + 1], None)
            m_all = jnp.where(crow == c, m_new, m_all)
            issue_scores(t + 1, c)
        return m_all

    npair = i // kpt
    m_all = lax.fori_loop(0, npair, body, jnp.full((nc, blk), NEG_BIG, F32))

    lam = (jnp.exp(jnp.sum(lq1_ref[...] * lk1_ref[...], axis=1, keepdims=True))
           - jnp.exp(jnp.sum(lq2_ref[...] * lk2_ref[...], axis=1, keepdims=True)) + lam_init)

    def finish(nb, keep):
        for hh in range(ATT_HPS):
            for c in (2 * hh, 2 * hh + 1):
                softmax_values(kpt * npair, nb, c, m_all[c:c + 1], keep)
                issue_scores(0, c, qs_next)
            a1, a2 = acc_ref[2 * hh], acc_ref[2 * hh + 1]
            o = (a1[:DIFF_DV] / a1[DIFF_DV:DIFF_DV + 1]
                 - lam * (a2[:DIFF_DV] / a2[DIFF_DV:DIFF_DV + 1]))
            msq = jnp.mean(o * o, axis=0, keepdims=True)
            o = o * lax.rsqrt(msq + EPS) * nw_ref[...] * (1.0 - lam_init)
            o_ref[:, hh * DIFF_DV:(hh + 1) * DIFF_DV] = o.T.astype(BF16)

    kr = lax.broadcasted_iota(jnp.int32, (kpt * blk, blk), 0)
    qc = lax.broadcasted_iota(jnp.int32, (kpt * blk, blk), 1)

    @pl.when(i % kpt == 0)
    def _():
        finish(1, (kr <= qc)[:blk])

    @pl.when(i % kpt == 1)
    def _():
        finish(2, (kr < blk) | (kr - blk <= qc))


def _attn(dqt, dk, dvt, lq1, lk1, lq2, lk2, nw_col, B, S, lam_init):
    T = B * S
    blk = TOK_BLK
    nblk = S // blk
    H, hps = DIFF_HEADS, ATT_HPS
    qt5 = dqt.reshape(B, nblk, H, 2 * DIFF_DH, blk)
    vt5 = dvt.reshape(B, nblk, H, DIFF_DV, blk)
    k3 = dk.reshape(B, S, DIFF_QK)
    vec = pl.BlockSpec((1, DIFF_DH), lambda b, h, i: (0, 0))
    out = pl.pallas_call(
        functools.partial(_attn_body, lam_init=lam_init),
        grid=(B, H // hps, nblk),
        in_specs=[
            pl.BlockSpec((None, None, hps, 2 * DIFF_DH, blk), lambda b, h, i: (b, i, h, 0, 0)),
            pl.BlockSpec((None, None, hps, 2 * DIFF_DH, blk),
                         lambda b, h, i: (b, jnp.minimum(i + 1, nblk - 1), h, 0, 0)),
            pl.BlockSpec((None, S, hps * 2 * DIFF_DH), lambda b, h, i: (b, 0, h)),
            pl.BlockSpec((None, nblk, hps, DIFF_DV, blk), lambda b, h, i: (b, 0, h, 0, 0)),
            vec, vec, vec, vec,
            pl.BlockSpec((DIFF_DV, 1), lambda b, h, i: (0, 0)),
        ],
        out_specs=pl.BlockSpec((None, blk, hps * DIFF_DV), lambda b, h, i: (b, i, h)),
        out_shape=jax.ShapeDtypeStruct((B, S, DIFF_V), BF16),
        scratch_shapes=[pltpu.VMEM((hps * 2, DIFF_DV + ATT_ONES_ROWS, blk), F32),
                        pltpu.VMEM((hps * 2, ATT_KPT * blk, blk), F32)],
        compiler_params=pltpu.CompilerParams(
            dimension_semantics=("arbitrary", "arbitrary", "arbitrary"), vmem_limit_bytes=VMEM_LIMIT),
        name="attn",
    )(qt5, qt5, k3, vt5, lq1, lk1, lq2, lk2, nw_col)
    return out.reshape(T, DIFF_V)


def _merge_body(x_ref, oa_ref, ob_ref, ga_ref, gb_ref, wa_ref, wb_ref, wo_ref, pw_ref, o_ref):
    ya = _dot(oa_ref[...], wa_ref[...])
    yb = _dot(ob_ref[...], wb_ref[...])
    mix = ga_ref[...].astype(F32) * ya + gb_ref[...].astype(F32) * yb
    mixed = _dot(mix.astype(BF16), wo_ref[...])
    o_ref[...] = x_ref[...] + _rms(mixed, pw_ref[...])


def _merge(x2, oa, ob, gates, wa, wb, wo, pw):
    T = x2.shape[0]
    tm = MERGE_TM
    row = lambda i: (i, 0)
    return pl.pallas_call(
        _merge_body,
        grid=(T // tm,),
        in_specs=[
            pl.BlockSpec((tm, D_MODEL), row),
            pl.BlockSpec((tm, GLA_V), row),
            pl.BlockSpec((tm, DIFF_V), row),
            pl.BlockSpec((tm, D_MODEL), lambda i: (i, 0)),
            pl.BlockSpec((tm, D_MODEL), lambda i: (i, 1)),
            _const_spec((GLA_V, D_MODEL)),
            _const_spec((DIFF_V, D_MODEL)),
            _const_spec((D_MODEL, D_MODEL)),
            _const_spec((1, D_MODEL)),
        ],
        out_specs=pl.BlockSpec((tm, D_MODEL), row),
        out_shape=jax.ShapeDtypeStruct((T, D_MODEL), F32),
        compiler_params=pltpu.CompilerParams(
            dimension_semantics=("parallel",), vmem_limit_bytes=VMEM_LIMIT),
        name="merge",
    )(x2, oa, ob, gates, gates, wa, wb, wo, pw)


def _mlp_body(x_ref, wu_ref, wd_ref, prew_ref, postw_ref, o_ref):
    x = x_ref[...]
    h = _rms(x, prew_ref[...]).astype(BF16)
    f = None
    for c in range(D_FF // FF_CHUNK):
        a = jnp.maximum(_dot(h, wu_ref[:, c * FF_CHUNK:(c + 1) * FF_CHUNK]), 0.0)
        part = _dot((a * a).astype(BF16), wd_ref[c * FF_CHUNK:(c + 1) * FF_CHUNK, :])
        f = part if f is None else f + part
    o_ref[...] = x + _rms(f, postw_ref[...])


def _mlp(x1, wu, wd, prew, postw):
    T = x1.shape[0]
    tm = MLP_TM
    row = lambda i: (i, 0)
    return pl.pallas_call(
        _mlp_body,
        grid=(T // tm,),
        in_specs=[
            pl.BlockSpec((tm, D_MODEL), row),
            _const_spec((D_MODEL, D_FF)),
            _const_spec((D_FF, D_MODEL)),
            _const_spec((1, D_MODEL)),
            _const_spec((1, D_MODEL)),
        ],
        out_specs=pl.BlockSpec((tm, D_MODEL), row),
        out_shape=jax.ShapeDtypeStruct((T, D_MODEL), F32),
        compiler_params=pltpu.CompilerParams(
            dimension_semantics=("parallel",), vmem_limit_bytes=VMEM_LIMIT),
        name="mlp",
    )(x1, wu, wd, prew, postw)


def _rope_tables(positions):
    T = positions.size
    inv_freq = ROPE_THETA ** (-jnp.arange(0, ROT_DIM, 2, dtype=F32) / ROT_DIM)
    ang = positions.astype(F32).reshape(T, 1) * inv_freq
    cos, sin = jnp.cos(ang), jnp.sin(ang)
    cs = jnp.concatenate([cos, sin], axis=1)
    ct = cos.reshape(T // TOK_BLK, TOK_BLK, ROT_HALF).transpose(0, 2, 1)
    st = sin.reshape(T // TOK_BLK, TOK_BLK, ROT_HALF).transpose(0, 2, 1)
    return cs, ct, st


def _rope_expand_matrix():
    import numpy as np
    e = np.zeros((2 * ROT_HALF, 2 * LANES), np.float32)
    for base in range(0, LANES, DIFF_DH):
        for j in range(ROT_HALF):
            e[j, base + j] = 1.0
            e[j, base + ROT_HALF + j] = 1.0
            e[ROT_HALF + j, LANES + base + j] = -1.0
            e[ROT_HALF + j, LANES + base + ROT_HALF + j] = 1.0
    return jnp.asarray(e, BF16)


def kernel(x, positions, w_in, b_gate, w_gk_up, b_gk, gla_norm_w, lambda_q1, lambda_k1, lambda_q2,
           lambda_k2, diff_norm_w, w_branch_a, w_branch_b, w_out, pre_mix_w, post_mix_w, pre_mlp_w,
           post_mlp_w, w_up, w_down):
    B, S, D = x.shape
    T = B * S
    depth = w_in.shape[0]
    cs, ct, st = _rope_tables(positions)
    rope_e = _rope_expand_matrix()
    offs = [0]
    for n in IN_SIZES:
        offs.append(offs[-1] + n)
    (o_gq, o_gk, o_gv, o_og, o_low, o_dq, o_dk, o_dv, o_gates, o_end) = offs

    x2 = x.reshape(T, D)
    for l in range(depth):
        lam_init = 0.8 - 0.6 * math.exp(-0.3 * l)
        w = w_in[l]
        wnat = jnp.concatenate(
            [w[:, o_gq:o_gk], w[:, o_gk:o_gv], w[:, o_gv:o_og], w[:, o_og:o_low],
             w[:, o_dk:o_dv], w[:, o_gates:o_end]], axis=1).astype(BF16)
        wlow = jnp.pad(w[:, o_low:o_dq], ((0, 0), (0, MXU_COLS - GLA_RANK))).astype(BF16)
        wgk = jnp.pad(w_gk_up[l], ((0, MXU_COLS - GLA_RANK), (0, 0))).astype(BF16)
        wt = jnp.concatenate([w[:, o_dq:o_dk], w[:, o_dv:o_gates]], axis=1).T.astype(BF16)

        gq, gk, gv, og, la, dk, gates, dqt, dvt = _inproj(
            x2, pre_mix_w[l].reshape(1, D), wnat, wlow, wgk, b_gk[l].reshape(1, -1),
            b_gate[l].reshape(1, -1), wt, cs, rope_e, ct, st)

        o_a = _gla(gq, gk, gv, la, og, gla_norm_w[l].reshape(1, -1), B, S)
        o_b = _attn(dqt, dk, dvt,
                    lambda_q1[l].reshape(1, -1), lambda_k1[l].reshape(1, -1),
                    lambda_q2[l].reshape(1, -1), lambda_k2[l].reshape(1, -1),
                    diff_norm_w[l].reshape(-1, 1), B, S, lam_init)

        x1 = _merge(x2, o_a, o_b, gates, w_branch_a[l].astype(BF16), w_branch_b[l].astype(BF16),
                    w_out[l].astype(BF16), post_mix_w[l].reshape(1, D))
        x2 = _mlp(x1, w_up[l].astype(BF16), w_down[l].astype(BF16),
                  pre_mlp_w[l].reshape(1, D), post_mlp_w[l].reshape(1, D))
    return x2.reshape(B, S, D)
```

```python
import functools
import math

import jax
import jax.numpy as jnp
from jax import lax
from jax.experimental import pallas as pl
from jax.experimental.pallas import tpu as pltpu

F32 = jnp.float32
BF16 = jnp.bfloat16

D_MODEL = 1024
GLA_HEADS = 4
GLA_DK = 128
GLA_DV = 256
GLA_RANK = 16
GLA_GATE_NORM = 16.0
GLA_CHUNK = 64
DIFF_HEADS = 8
DIFF_DH = 64
DIFF_DV = 2 * DIFF_DH
ROT_DIM = DIFF_DH // 4
ROT_HALF = ROT_DIM // 2
ROPE_THETA = 500000.0
D_FF = 4 * D_MODEL
EPS = 1e-6

GLA_QK = GLA_HEADS * GLA_DK
GLA_V = GLA_HEADS * GLA_DV
DIFF_QK = DIFF_HEADS * 2 * DIFF_DH
DIFF_V = DIFF_HEADS * DIFF_DV
IN_SIZES = (GLA_QK, GLA_QK, GLA_V, GLA_V, GLA_RANK, DIFF_QK, DIFF_QK, DIFF_V, 2 * D_MODEL)

LANES = 128
MXU_COLS = 256
TOK_BLK = 256
GLA_SUPER = 2 * GLA_CHUNK
GLA_BPS = 4
ATT_HPS = 8
ATT_KPT = 2
ATT_ONES_ROWS = 16
LOG2E = math.log2(math.e)
INPROJ_TM = 512
MERGE_TM = 1024
MLP_TM = 1024
FF_CHUNK = 1024
VMEM_LIMIT = 56 * 1024 * 1024
NEG_BIG = -1e30


def _rms(xf, w):
    ms = jnp.mean(xf * xf, axis=-1, keepdims=True)
    return xf * lax.rsqrt(ms + EPS) * w


def _dot(a, b):
    return jnp.dot(a, b, preferred_element_type=F32)


def _dot_nt(a, b):
    return lax.dot_general(a, b, (((1,), (1,)), ((), ())), preferred_element_type=F32)


def _const_spec(shape):
    nd = len(shape)
    return pl.BlockSpec(shape, lambda *_: (0,) * nd, pipeline_mode=pl.Buffered(1))


_N_GQ, _N_GK, _N_GV, _N_OG, _N_DK, _N_GATE = 0, 512, 1024, 2048, 3072, 4096
_N_END = 6144
_T_DQ, _T_DV, _T_END = 0, 1024, 2048


def _inproj_body(x_ref, pmw_ref, wnat_ref, wlow_ref, wgk_ref, bgk_ref, bgate_ref, wt_ref,
                 cs_ref, rope_e_ref, ct_ref, st_ref,
                 gq_ref, gk_ref, gv_ref, og_ref, la_ref, dk_ref, gates_ref,
                 dqt_ref, dvt_ref):
    u = _rms(x_ref[...], pmw_ref[...]).astype(BF16)

    def mm(lo, hi):
        return _dot(u, wnat_ref[:, lo:hi])

    og = mm(_N_OG, _N_DK)
    og_ref[...] = (og * jax.nn.sigmoid(og)).astype(BF16)

    g_low = _dot(u, wlow_ref[...]).astype(BF16)
    z = _dot(g_low, wgk_ref[...]) + bgk_ref[...]
    la_ref[...] = (jnp.minimum(z, 0.0) - jnp.log(1.0 + jnp.exp(-jnp.abs(z)))) * (1.0 / GLA_GATE_NORM)

    dk = mm(_N_DK, _N_GATE)
    cs = cs_ref[...]
    cs_hi = cs.astype(BF16)
    cs_lo = (cs - cs_hi.astype(F32)).astype(BF16)
    tab = _dot(cs_hi, rope_e_ref[...]) + _dot(cs_lo, rope_e_ref[...])
    lane = lax.broadcasted_iota(jnp.int32, (1, LANES), 1) % DIFF_DH
    c = tab[:, :LANES] + jnp.where(lane >= ROT_DIM, 1.0, 0.0)
    sgn_sin = tab[:, LANES:]
    first_half = lane < ROT_HALF
    for h in range(DIFF_HEADS):
        t = dk[:, h * LANES:(h + 1) * LANES]
        partner = jnp.where(first_half, pltpu.roll(t, LANES - ROT_HALF, 1), pltpu.roll(t, ROT_HALF, 1))
        dk_ref[:, h * LANES:(h + 1) * LANES] = (t * c + partner * sgn_sin).astype(BF16)

    gates_ref[...] = jax.nn.sigmoid(mm(_N_GATE, _N_END) + bgate_ref[...]).astype(BF16)

    def mmt(lo, hi):
        return _dot_nt(wt_ref[lo:hi, :], u)

    nsub = dvt_ref.shape[0]
    dv = mmt(_T_DV, _T_END).astype(BF16)
    dq = mmt(_T_DQ, _T_DV) * (DIFF_DH ** -0.5 * LOG2E)
    for r in range(nsub):
        tok = slice(r * TOK_BLK, (r + 1) * TOK_BLK)
        dvt_ref[r] = dv[:, tok]
        ct, st = ct_ref[r], st_ref[r]
        pieces = []
        for g in range(DIFF_HEADS * 2):
            b = g * DIFF_DH
            t1, t2 = dq[b:b + ROT_HALF, tok], dq[b + ROT_HALF:b + ROT_DIM, tok]
            pieces += [t1 * ct - t2 * st, t2 * ct + t1 * st, dq[b + ROT_DIM:b + DIFF_DH, tok]]
        dqt_ref[r] = jnp.concatenate(pieces, axis=0).astype(BF16)

    gq_ref[...] = (mm(_N_GQ, _N_GK) * (GLA_DK ** -0.5)).astype(BF16)
    gk_ref[...] = mm(_N_GK, _N_GV).astype(BF16)
    gv_ref[...] = mm(_N_GV, _N_OG).astype(BF16)


def _inproj(x2, pmw, wnat, wlow, wgk, bgk, bgate, wt, cs, rope_e, ct, st):
    T = x2.shape[0]
    tm = INPROJ_TM
    blk = TOK_BLK
    nsub = tm // blk
    nblk = T // blk
    row = lambda i: (i, 0)
    blk3 = lambda i: (i, 0, 0)
    in_specs = [
        pl.BlockSpec((tm, D_MODEL), row),
        _const_spec((1, D_MODEL)),
        _const_spec((D_MODEL, _N_END)),
        _const_spec((D_MODEL, MXU_COLS)),
        _const_spec((MXU_COLS, GLA_QK)),
        _const_spec((1, GLA_QK)),
        _const_spec((1, 2 * D_MODEL)),
        _const_spec((_T_END, D_MODEL)),
        pl.BlockSpec((tm, 2 * ROT_HALF), row),
        _const_spec((2 * ROT_HALF, 2 * LANES)),
        pl.BlockSpec((nsub, ROT_HALF, blk), blk3),
        pl.BlockSpec((nsub, ROT_HALF, blk), blk3),
    ]
    out_shape = [
        jax.ShapeDtypeStruct((T, GLA_QK), BF16),
        jax.ShapeDtypeStruct((T, GLA_QK), BF16),
        jax.ShapeDtypeStruct((T, GLA_V), BF16),
        jax.ShapeDtypeStruct((T, GLA_V), BF16),
        jax.ShapeDtypeStruct((T, GLA_QK), F32),
        jax.ShapeDtypeStruct((T, DIFF_QK), BF16),
        jax.ShapeDtypeStruct((T, 2 * D_MODEL), BF16),
        jax.ShapeDtypeStruct((nblk, DIFF_QK, blk), BF16),
        jax.ShapeDtypeStruct((nblk, DIFF_V, blk), BF16),
    ]
    out_specs = [
        pl.BlockSpec((tm, GLA_QK), row),
        pl.BlockSpec((tm, GLA_QK), row),
        pl.BlockSpec((tm, GLA_V), row),
        pl.BlockSpec((tm, GLA_V), row),
        pl.BlockSpec((tm, GLA_QK), row),
        pl.BlockSpec((tm, DIFF_QK), row),
        pl.BlockSpec((tm, 2 * D_MODEL), row),
        pl.BlockSpec((nsub, DIFF_QK, blk), blk3),
        pl.BlockSpec((nsub, DIFF_V, blk), blk3),
    ]
    return pl.pallas_call(
        _inproj_body,
        grid=(T // tm,),
        in_specs=in_specs,
        out_specs=out_specs,
        out_shape=out_shape,
        compiler_params=pltpu.CompilerParams(
            dimension_semantics=("parallel",), vmem_limit_bytes=VMEM_LIMIT),
        name="inproj",
    )(x2, pmw, wnat, wlow, wgk, bgk, bgate, wt, cs, rope_e, ct, st)


def _gla_body(q_ref, k_ref, v_ref, la_ref, og_ref, nw_ref, o_ref, state_ref):
    @pl.when(pl.program_id(1) == 0)
    def _():
        state_ref[...] = jnp.zeros_like(state_ref)

    n = GLA_SUPER
    ri = lax.broadcasted_iota(jnp.int32, (n, n), 0)
    ci = lax.broadcasted_iota(jnp.int32, (n, n), 1)
    tri = (ri >= ci).astype(BF16)
    same_chunk_causal = ((ri >= GLA_CHUNK) == (ci >= GLA_CHUNK)) & (ci <= ri)
    row = lax.broadcasted_iota(jnp.int32, (n, GLA_DK), 0)
    second = row >= GLA_CHUNK
    nw = nw_ref[...]

    for sc, bb in [(sc, bb) for sc in range(TOK_BLK // n) for bb in range(GLA_BPS)]:
        r0 = sc * n
        la = la_ref[bb, r0:r0 + n, :]
        hi = la.astype(BF16)
        lo = (la - hi.astype(F32)).astype(BF16)
        g_all = _dot(tri, hi) + _dot(tri, lo)

        pend = []
        for h in range(GLA_HEADS):
            ks = slice(h * GLA_DK, (h + 1) * GLA_DK)
            vs = slice(h * GLA_DV, (h + 1) * GLA_DV)
            g = g_all[:, ks]
            q = q_ref[bb, r0:r0 + n, ks].astype(F32)
            k = k_ref[bb, r0:r0 + n, ks].astype(F32)
            v = v_ref[bb, r0:r0 + n, vs]
            g_mid = jnp.where(second, g[GLA_CHUNK + GLA_CHUNK // 2:GLA_CHUNK + GLA_CHUNK // 2 + 1],
                              g[GLA_CHUNK // 2:GLA_CHUNK // 2 + 1])
            g_c0 = g[GLA_CHUNK - 1:GLA_CHUNK]
            g_last = g[n - 1:n]

            qi = (q * jnp.exp(g - g_mid)).astype(BF16)
            kj = (k * jnp.exp(g_mid - g)).astype(BF16)
            a_diag = _dot_nt(qi, kj)
            qo = jnp.where(second, q * jnp.exp(jnp.minimum(g - g_c0, 0.0)), 0.0).astype(BF16)
            ko = jnp.where(second, 0.0, k * jnp.exp(jnp.minimum(g_c0 - g, 0.0))).astype(BF16)
            a = (jnp.where(same_chunk_causal, a_diag, 0.0) + _dot_nt(qo, ko)).astype(BF16)

            st = state_ref[bb, h]
            o_inter = _dot((q * jnp.exp(g)).astype(BF16), st.astype(BF16))
            gt = g.T
            g_last_col = gt[:, n - 1:n]
            k_st_t = (k.T * jnp.exp(g_last_col - gt)).astype(BF16)
            state_ref[bb, h] = st * jnp.exp(g_last_col) + _dot(k_st_t, v)
            pend.append((a, v, o_inter, vs))

        for a, v, o_inter, vs in pend:
            o = _dot(a, v) + o_inter
            og = og_ref[bb, r0:r0 + n, vs].astype(F32)
            o_ref[bb, r0:r0 + n, vs] = (_rms(o, nw) * og).astype(BF16)


def _gla(gq, gk, gv, la, og, nw, B, S):
    T = B * S
    tb = TOK_BLK
    bps = GLA_BPS
    blkmap = lambda b, i: (b, i, 0)
    r3 = lambda a: a.reshape(B, S, a.shape[-1])
    out = pl.pallas_call(
        _gla_body,
        grid=(B // bps, S // tb),
        in_specs=[
            pl.BlockSpec((bps, tb, GLA_QK), blkmap),
            pl.BlockSpec((bps, tb, GLA_QK), blkmap),
            pl.BlockSpec((bps, tb, GLA_V), blkmap),
            pl.BlockSpec((bps, tb, GLA_QK), blkmap),
            pl.BlockSpec((bps, tb, GLA_V), blkmap),
            pl.BlockSpec((1, GLA_DV), lambda b, i: (0, 0)),
        ],
        out_specs=pl.BlockSpec((bps, tb, GLA_V), blkmap),
        out_shape=jax.ShapeDtypeStruct((B, S, GLA_V), BF16),
        scratch_shapes=[pltpu.VMEM((bps, GLA_HEADS, GLA_DK, GLA_DV), F32)],
        compiler_params=pltpu.CompilerParams(
            dimension_semantics=("parallel", "arbitrary"), vmem_limit_bytes=VMEM_LIMIT),
        name="gla",
    )(r3(gq), r3(gk), r3(gv), r3(la), r3(og), nw)
    return out.reshape(T, GLA_V)


def _attn_body(qt_ref, qtn_ref, k_ref, vt_ref, lq1_ref, lk1_ref, lq2_ref, lk2_ref, nw_ref, o_ref,
               acc_ref, s_ref, *, lam_init):
    i = pl.program_id(2)
    blk = TOK_BLK
    chains = [(hh, sub) for hh in range(ATT_HPS) for sub in range(2)]
    nc = len(chains)

    def split_subheads(ref):
        out = []
        for hh in range(ATT_HPS):
            qt = ref[hh]
            frow = lax.broadcasted_iota(jnp.int32, qt.shape, 0)
            out.append((jnp.where(frow < DIFF_DH, qt, jnp.zeros_like(qt)),
                        jnp.where(frow >= DIFF_DH, qt, jnp.zeros_like(qt))))
        return out

    qs = split_subheads(qt_ref)
    qs_next = split_subheads(qtn_ref)

    kpt = ATT_KPT

    def issue_scores(jp, c, q=None):
        hh, sub = chains[c]
        kb = k_ref[pl.ds(pl.multiple_of(jp * (kpt * blk), kpt * blk), kpt * blk),
                   hh * 2 * DIFF_DH:(hh + 1) * 2 * DIFF_DH]
        s_ref[c] = _dot(kb, (qs if q is None else q)[hh][sub])

    def softmax_values(jb, nb, c, m, keep):
        hh, sub = chains[c]
        s = s_ref[c, :nb * blk]
        if keep is not None:
            s = jnp.where(keep, s, NEG_BIG)
        m_new = jnp.maximum(m, jnp.max(s, axis=0, keepdims=True))
        p = jnp.exp2(s - m_new).astype(BF16)
        vt = jnp.concatenate([vt_ref[jb + r, hh] for r in range(nb)], axis=1)
        vx = jnp.concatenate([vt, jnp.ones((ATT_ONES_ROWS, nb * blk), BF16)], axis=0)
        acc_ref[c] = acc_ref[c] * jnp.exp2(m - m_new) + _dot(vx, p)
        return m_new

    acc_ref[...] = jnp.zeros_like(acc_ref)

    @pl.when(i == 0)
    def _():
        for c in range(nc):
            issue_scores(0, c)

    crow = lax.broadcasted_iota(jnp.int32, (nc, blk), 0)

    def body(t, m_all):
        for c in range(nc):
            m_new = softmax_values(kpt * t, kpt, c, m_all[c:c + 1], None)
            m_all = jnp.where(crow == c, m_new, m_all)
            issue_scores(t + 1, c)
        return m_all

    npair = i // kpt
    m_all = lax.fori_loop(0, npair, body, jnp.full((nc, blk), NEG_BIG, F32))

    lam = (jnp.exp(jnp.sum(lq1_ref[...] * lk1_ref[...], axis=1, keepdims=True))
           - jnp.exp(jnp.sum(lq2_ref[...] * lk2_ref[...], axis=1, keepdims=True)) + lam_init)

    def finish(nb, keep):
        for hh in range(ATT_HPS):
            for c in (2 * hh, 2 * hh + 1):
                softmax_values(kpt * npair, nb, c, m_all[c:c + 1], keep)
                issue_scores(0, c, qs_next)
            a1, a2 = acc_ref[2 * hh], acc_ref[2 * hh + 1]
            o = (a1[:DIFF_DV] / a1[DIFF_DV:DIFF_DV + 1]
                 - lam * (a2[:DIFF_DV] / a2[DIFF_DV:DIFF_DV + 1]))
            msq = jnp.mean(o * o, axis=0, keepdims=True)
            o = o * lax.rsqrt(msq + EPS) * nw_ref[...] * (1.0 - lam_init)
            o_ref[:, hh * DIFF_DV:(hh + 1) * DIFF_DV] = o.T.astype(BF16)

    kr = lax.broadcasted_iota(jnp.int32, (kpt * blk, blk), 0)
    qc = lax.broadcasted_iota(jnp.int32, (kpt * blk, blk), 1)

    @pl.when(i % kpt == 0)
    def _():
        finish(1, (kr <= qc)[:blk])

    @pl.when(i % kpt == 1)
    def _():
        finish(2, (kr < blk) | (kr - blk <= qc))


def _attn(dqt, dk, dvt, lq1, lk1, lq2, lk2, nw_col, B, S, lam_init):
    T = B * S
    blk = TOK_BLK
    nblk = S // blk
    H, hps = DIFF_HEADS, ATT_HPS
    qt5 = dqt.reshape(B, nblk, H, 2 * DIFF_DH, blk)
    vt5 = dvt.reshape(B, nblk, H, DIFF_DV, blk)
    k3 = dk.reshape(B, S, DIFF_QK)
    vec = pl.BlockSpec((1, DIFF_DH), lambda b, h, i: (0, 0))
    out = pl.pallas_call(
        functools.partial(_attn_body, lam_init=lam_init),
        grid=(B, H // hps, nblk),
        in_specs=[
            pl.BlockSpec((None, None, hps, 2 * DIFF_DH, blk), lambda b, h, i: (b, i, h, 0, 0)),
            pl.BlockSpec((None, None, hps, 2 * DIFF_DH, blk),
                         lambda b, h, i: (b, jnp.minimum(i + 1, nblk - 1), h, 0, 0)),
            pl.BlockSpec((None, S, hps * 2 * DIFF_DH), lambda b, h, i: (b, 0, h)),
            pl.BlockSpec((None, nblk, hps, DIFF_DV, blk), lambda b, h, i: (b, 0, h, 0, 0)),
            vec, vec, vec, vec,
            pl.BlockSpec((DIFF_DV, 1), lambda b, h, i: (0, 0)),
        ],
        out_specs=pl.BlockSpec((None, blk, hps * DIFF_DV), lambda b, h, i: (b, i, h)),
        out_shape=jax.ShapeDtypeStruct((B, S, DIFF_V), BF16),
        scratch_shapes=[pltpu.VMEM((hps * 2, DIFF_DV + ATT_ONES_ROWS, blk), F32),
                        pltpu.VMEM((hps * 2, ATT_KPT * blk, blk), F32)],
        compiler_params=pltpu.CompilerParams(
            dimension_semantics=("arbitrary", "arbitrary", "arbitrary"), vmem_limit_bytes=VMEM_LIMIT),
        name="attn",
    )(qt5, qt5, k3, vt5, lq1, lk1, lq2, lk2, nw_col)
    return out.reshape(T, DIFF_V)


def _merge_body(x_ref, oa_ref, ob_ref, ga_ref, gb_ref, wa_ref, wb_ref, wo_ref, pw_ref, o_ref):
    ya = _dot(oa_ref[...], wa_ref[...])
    yb = _dot(ob_ref[...], wb_ref[...])
    mix = ga_ref[...].astype(F32) * ya + gb_ref[...].astype(F32) * yb
    mixed = _dot(mix.astype(BF16), wo_ref[...])
    o_ref[...] = x_ref[...] + _rms(mixed, pw_ref[...])


def _merge(x2, oa, ob, gates, wa, wb, wo, pw):
    T = x2.shape[0]
    tm = MERGE_TM
    row = lambda i: (i, 0)
    return pl.pallas_call(
        _merge_body,
        grid=(T // tm,),
        in_specs=[
            pl.BlockSpec((tm, D_MODEL), row),
            pl.BlockSpec((tm, GLA_V), row),
            pl.BlockSpec((tm, DIFF_V), row),
            pl.BlockSpec((tm, D_MODEL), lambda i: (i, 0)),
            pl.BlockSpec((tm, D_MODEL), lambda i: (i, 1)),
            _const_spec((GLA_V, D_MODEL)),
            _const_spec((DIFF_V, D_MODEL)),
            _const_spec((D_MODEL, D_MODEL)),
            _const_spec((1, D_MODEL)),
        ],
        out_specs=pl.BlockSpec((tm, D_MODEL), row),
        out_shape=jax.ShapeDtypeStruct((T, D_MODEL), F32),
        compiler_params=pltpu.CompilerParams(
            dimension_semantics=("parallel",), vmem_limit_bytes=VMEM_LIMIT),
        name="merge",
    )(x2, oa, ob, gates, gates, wa, wb, wo, pw)


def _mlp_body(x_ref, wu_ref, wd_ref, prew_ref, postw_ref, o_ref):
    x = x_ref[...]
    h = _rms(x, prew_ref[...]).astype(BF16)
    f = None
    for c in range(D_FF // FF_CHUNK):
        a = jnp.maximum(_dot(h, wu_ref[:, c * FF_CHUNK:(c + 1) * FF_CHUNK]), 0.0)
        part = _dot((a * a).astype(BF16), wd_ref[c * FF_CHUNK:(c + 1) * FF_CHUNK, :])
        f = part if f is None else f + part
    o_ref[...] = x + _rms(f, postw_ref[...])


def _mlp(x1, wu, wd, prew, postw):
    T = x1.shape[0]
    tm = MLP_TM
    row = lambda i: (i, 0)
    return pl.pallas_call(
        _mlp_body,
        grid=(T // tm,),
        in_specs=[
            pl.BlockSpec((tm, D_MODEL), row),
            _const_spec((D_MODEL, D_FF)),
            _const_spec((D_FF, D_MODEL)),
            _const_spec((1, D_MODEL)),
            _const_spec((1, D_MODEL)),
        ],
        out_specs=pl.BlockSpec((tm, D_MODEL), row),
        out_shape=jax.ShapeDtypeStruct((T, D_MODEL), F32),
        compiler_params=pltpu.CompilerParams(
            dimension_semantics=("parallel",), vmem_limit_bytes=VMEM_LIMIT),
        name="mlp",
    )(x1, wu, wd, prew, postw)


def _rope_tables(positions):
    T = positions.size
    inv_freq = ROPE_THETA ** (-jnp.arange(0, ROT_DIM, 2, dtype=F32) / ROT_DIM)
    ang = positions.astype(F32).reshape(T, 1) * inv_freq
    cos, sin = jnp.cos(ang), jnp.sin(ang)
    cs = jnp.concatenate([cos, sin], axis=1)
    ct = cos.reshape(T // TOK_BLK, TOK_BLK, ROT_HALF).transpose(0, 2, 1)
    st = sin.reshape(T // TOK_BLK, TOK_BLK, ROT_HALF).transpose(0, 2, 1)
    return cs, ct, st


def _rope_expand_matrix():
    import numpy as np
    e = np.zeros((2 * ROT_HALF, 2 * LANES), np.float32)
    for base in range(0, LANES, DIFF_DH):
        for j in range(ROT_HALF):
            e[j, base + j] = 1.0
            e[j, base + ROT_HALF + j] = 1.0
            e[ROT_HALF + j, LANES + base + j] = -1.0
            e[ROT_HALF + j, LANES + base + ROT_HALF + j] = 1.0
    return jnp.asarray(e, BF16)


def kernel(x, positions, w_in, b_gate, w_gk_up, b_gk, gla_norm_w, lambda_q1, lambda_k1, lambda_q2,
           lambda_k2, diff_norm_w, w_branch_a, w_branch_b, w_out, pre_mix_w, post_mix_w, pre_mlp_w,
           post_mlp_w, w_up, w_down):
    B, S, D = x.shape
    T = B * S
    depth = w_in.shape[0]
    cs, ct, st = _rope_tables(positions)
    rope_e = _rope_expand_matrix()
    offs = [0]
    for n in IN_SIZES:
        offs.append(offs[-1] + n)
    (o_gq, o_gk, o_gv, o_og, o_low, o_dq, o_dk, o_dv, o_gates, o_end) = offs

    x2 = x.reshape(T, D)
    for l in range(depth):
        lam_init = 0.8 - 0.6 * math.exp(-0.3 * l)
        w = w_in[l]
        wnat = jnp.concatenate(
            [w[:, o_gq:o_gk], w[:, o_gk:o_gv], w[:, o_gv:o_og], w[:, o_og:o_low],
             w[:, o_dk:o_dv], w[:, o_gates:o_end]], axis=1).astype(BF16)
        wlow = jnp.pad(w[:, o_low:o_dq], ((0, 0), (0, MXU_COLS - GLA_RANK))).astype(BF16)
        wgk = jnp.pad(w_gk_up[l], ((0, MXU_COLS - GLA_RANK), (0, 0))).astype(BF16)
        wt = jnp.concatenate([w[:, o_dq:o_dk], w[:, o_dv:o_gates]], axis=1).T.astype(BF16)

        gq, gk, gv, og, la, dk, gates, dqt, dvt = _inproj(
            x2, pre_mix_w[l].reshape(1, D), wnat, wlow, wgk, b_gk[l].reshape(1, -1),
            b_gate[l].reshape(1, -1), wt, cs, rope_e, ct, st)

        o_a = _gla(gq, gk, gv, la, og, gla_norm_w[l].reshape(1, -1), B, S)
        o_b = _attn(dqt, dk, dvt,
                    lambda_q1[l].reshape(1, -1), lambda_k1[l].reshape(1, -1),
                    lambda_q2[l].reshape(1, -1), lambda_k2[l].reshape(1, -1),
                    diff_norm_w[l].reshape(-1, 1), B, S, lam_init)

        x1 = _merge(x2, o_a, o_b, gates, w_branch_a[l].astype(BF16), w_branch_b[l].astype(BF16),
                    w_out[l].astype(BF16), post_mix_w[l].reshape(1, D))
        x2 = _mlp(x1, w_up[l].astype(BF16), w_down[l].astype(BF16),
                  pre_mlp_w[l].reshape(1, D), post_mlp_w[l].reshape(1, D))
    return x2.reshape(B, S, D)
```

```python
import functools
import math

import jax
import jax.numpy as jnp
from jax import lax
from jax.experimental import pallas as pl
from jax.experimental.pallas import tpu as pltpu

F32 = jnp.float32
BF16 = jnp.bfloat16

D_MODEL = 1024
GLA_HEADS = 4
GLA_DK = 128
GLA_DV = 256
GLA_RANK = 16
GLA_GATE_NORM = 16.0
GLA_CHUNK = 64
DIFF_HEADS = 8
DIFF_DH = 64
DIFF_DV = 2 * DIFF_DH
ROT_DIM = DIFF_DH // 4
ROT_HALF = ROT_DIM // 2
ROPE_THETA = 500000.0
D_FF = 4 * D_MODEL
EPS = 1e-6

GLA_QK = GLA_HEADS * GLA_DK
GLA_V = GLA_HEADS * GLA_DV
DIFF_QK = DIFF_HEADS * 2 * DIFF_DH
DIFF_V = DIFF_HEADS * DIFF_DV
IN_SIZES = (GLA_QK, GLA_QK, GLA_V, GLA_V, GLA_RANK, DIFF_QK, DIFF_QK, DIFF_V, 2 * D_MODEL)

LANES = 128
MXU_COLS = 256
TOK_BLK = 256
GLA_SUPER = 2 * GLA_CHUNK
GLA_BPS = 4
ATT_HPS = 8
ATT_KPT = 2
ATT_ONES_ROWS = 16
LOG2E = math.log2(math.e)
INPROJ_TM = 512
FUSED_TM = 512
MERGE_TM = 1024
MLP_TM = 1024
FF_CHUNK = 1024
VMEM_LIMIT = 56 * 1024 * 1024
NEG_BIG = -1e30


def _rms(xf, w):
    ms = jnp.mean(xf * xf, axis=-1, keepdims=True)
    return xf * lax.rsqrt(ms + EPS) * w


def _dot(a, b):
    return jnp.dot(a, b, preferred_element_type=F32)


def _dot_nt(a, b):
    return lax.dot_general(a, b, (((1,), (1,)), ((), ())), preferred_element_type=F32)


def _const_spec(shape):
    nd = len(shape)
    return pl.BlockSpec(shape, lambda *_: (0,) * nd, pipeline_mode=pl.Buffered(1))


_N_GQ, _N_GK, _N_GV, _N_OG, _N_DK, _N_GATE = 0, 512, 1024, 2048, 3072, 4096
_N_END = 6144
_T_DQ, _T_DV, _T_END = 0, 1024, 2048


def _inproj_body(x_ref, pmw_ref, wnat_ref, wlow_ref, wgk_ref, bgk_ref, bgate_ref, wt_ref,
                 cs_ref, rope_e_ref, ct_ref, st_ref,
                 gq_ref, gk_ref, gv_ref, og_ref, la_ref, dk_ref, gates_ref,
                 dqt_ref, dvt_ref):
    u = _rms(x_ref[...], pmw_ref[...]).astype(BF16)

    def mm(lo, hi):
        return _dot(u, wnat_ref[:, lo:hi])

    og = mm(_N_OG, _N_DK)
    og_ref[...] = (og * jax.nn.sigmoid(og)).astype(BF16)

    g_low = _dot(u, wlow_ref[...]).astype(BF16)
    z = _dot(g_low, wgk_ref[...]) + bgk_ref[...]
    la_ref[...] = (jnp.minimum(z, 0.0) - jnp.log(1.0 + jnp.exp(-jnp.abs(z)))) * (1.0 / GLA_GATE_NORM)

    dk = mm(_N_DK, _N_GATE)
    cs = cs_ref[...]
    cs_hi = cs.astype(BF16)
    cs_lo = (cs - cs_hi.astype(F32)).astype(BF16)
    tab = _dot(cs_hi, rope_e_ref[...]) + _dot(cs_lo, rope_e_ref[...])
    lane = lax.broadcasted_iota(jnp.int32, (1, LANES), 1) % DIFF_DH
    c = tab[:, :LANES] + jnp.where(lane >= ROT_DIM, 1.0, 0.0)
    sgn_sin = tab[:, LANES:]
    first_half = lane < ROT_HALF
    for h in range(DIFF_HEADS):
        t = dk[:, h * LANES:(h + 1) * LANES]
        partner = jnp.where(first_half, pltpu.roll(t, LANES - ROT_HALF, 1), pltpu.roll(t, ROT_HALF, 1))
        dk_ref[:, h * LANES:(h + 1) * LANES] = (t * c + partner * sgn_sin).astype(BF16)

    gates_ref[...] = jax.nn.sigmoid(mm(_N_GATE, _N_END) + bgate_ref[...]).astype(BF16)

    def mmt(lo, hi):
        return _dot_nt(wt_ref[lo:hi, :], u)

    nsub = dvt_ref.shape[0]
    dv = mmt(_T_DV, _T_END).astype(BF16)
    dq = mmt(_T_DQ, _T_DV) * (DIFF_DH ** -0.5 * LOG2E)
    for r in range(nsub):
        tok = slice(r * TOK_BLK, (r + 1) * TOK_BLK)
        dvt_ref[r] = dv[:, tok]
        ct, st = ct_ref[r], st_ref[r]
        pieces = []
        for g in range(DIFF_HEADS * 2):
            b = g * DIFF_DH
            t1, t2 = dq[b:b + ROT_HALF, tok], dq[b + ROT_HALF:b + ROT_DIM, tok]
            pieces += [t1 * ct - t2 * st, t2 * ct + t1 * st, dq[b + ROT_DIM:b + DIFF_DH, tok]]
        dqt_ref[r] = jnp.concatenate(pieces, axis=0).astype(BF16)

    gq_ref[...] = (mm(_N_GQ, _N_GK) * (GLA_DK ** -0.5)).astype(BF16)
    gk_ref[...] = mm(_N_GK, _N_GV).astype(BF16)
    gv_ref[...] = mm(_N_GV, _N_OG).astype(BF16)


def _inproj(x2, pmw, wnat, wlow, wgk, bgk, bgate, wt, cs, rope_e, ct, st):
    T = x2.shape[0]
    tm = INPROJ_TM
    blk = TOK_BLK
    nsub = tm // blk
    nblk = T // blk
    row = lambda i: (i, 0)
    blk3 = lambda i: (i, 0, 0)
    in_specs = [
        pl.BlockSpec((tm, D_MODEL), row),
        _const_spec((1, D_MODEL)),
        _const_spec((D_MODEL, _N_END)),
        _const_spec((D_MODEL, MXU_COLS)),
        _const_spec((MXU_COLS, GLA_QK)),
        _const_spec((1, GLA_QK)),
        _const_spec((1, 2 * D_MODEL)),
        _const_spec((_T_END, D_MODEL)),
        pl.BlockSpec((tm, 2 * ROT_HALF), row),
        _const_spec((2 * ROT_HALF, 2 * LANES)),
        pl.BlockSpec((nsub, ROT_HALF, blk), blk3),
        pl.BlockSpec((nsub, ROT_HALF, blk), blk3),
    ]
    out_shape = [
        jax.ShapeDtypeStruct((T, GLA_QK), BF16),
        jax.ShapeDtypeStruct((T, GLA_QK), BF16),
        jax.ShapeDtypeStruct((T, GLA_V), BF16),
        jax.ShapeDtypeStruct((T, GLA_V), BF16),
        jax.ShapeDtypeStruct((T, GLA_QK), F32),
        jax.ShapeDtypeStruct((T, DIFF_QK), BF16),
        jax.ShapeDtypeStruct((T, 2 * D_MODEL), BF16),
        jax.ShapeDtypeStruct((nblk, DIFF_QK, blk), BF16),
        jax.ShapeDtypeStruct((nblk, DIFF_V, blk), BF16),
    ]
    out_specs = [
        pl.BlockSpec((tm, GLA_QK), row),
        pl.BlockSpec((tm, GLA_QK), row),
        pl.BlockSpec((tm, GLA_V), row),
        pl.BlockSpec((tm, GLA_V), row),
        pl.BlockSpec((tm, GLA_QK), row),
        pl.BlockSpec((tm, DIFF_QK), row),
        pl.BlockSpec((tm, 2 * D_MODEL), row),
        pl.BlockSpec((nsub, DIFF_QK, blk), blk3),
        pl.BlockSpec((nsub, DIFF_V, blk), blk3),
    ]
    return pl.pallas_call(
        _inproj_body,
        grid=(T // tm,),
        in_specs=in_specs,
        out_specs=out_specs,
        out_shape=out_shape,
        compiler_params=pltpu.CompilerParams(
            dimension_semantics=("parallel",), vmem_limit_bytes=VMEM_LIMIT),
        name="inproj",
    )(x2, pmw, wnat, wlow, wgk, bgk, bgate, wt, cs, rope_e, ct, st)


def _gla_body(q_ref, k_ref, v_ref, la_ref, og_ref, nw_ref, o_ref, state_ref):
    @pl.when(pl.program_id(1) == 0)
    def _():
        state_ref[...] = jnp.zeros_like(state_ref)

    n = GLA_SUPER
    ri = lax.broadcasted_iota(jnp.int32, (n, n), 0)
    ci = lax.broadcasted_iota(jnp.int32, (n, n), 1)
    tri = (ri >= ci).astype(BF16)
    same_chunk_causal = ((ri >= GLA_CHUNK) == (ci >= GLA_CHUNK)) & (ci <= ri)
    row = lax.broadcasted_iota(jnp.int32, (n, GLA_DK), 0)
    second = row >= GLA_CHUNK
    nw = nw_ref[...]

    for sc, bb in [(sc, bb) for sc in range(TOK_BLK // n) for bb in range(GLA_BPS)]:
        r0 = sc * n
        la = la_ref[bb, r0:r0 + n, :]
        hi = la.astype(BF16)
        lo = (la - hi.astype(F32)).astype(BF16)
        g_all = _dot(tri, hi) + _dot(tri, lo)

        pend = []
        for h in range(GLA_HEADS):
            ks = slice(h * GLA_DK, (h + 1) * GLA_DK)
            vs = slice(h * GLA_DV, (h + 1) * GLA_DV)
            g = g_all[:, ks]
            q = q_ref[bb, r0:r0 + n, ks].astype(F32)
            k = k_ref[bb, r0:r0 + n, ks].astype(F32)
            v = v_ref[bb, r0:r0 + n, vs]
            g_mid = jnp.where(second, g[GLA_CHUNK + GLA_CHUNK // 2:GLA_CHUNK + GLA_CHUNK // 2 + 1],
                              g[GLA_CHUNK // 2:GLA_CHUNK // 2 + 1])
            g_c0 = g[GLA_CHUNK - 1:GLA_CHUNK]
            g_last = g[n - 1:n]

            qi = (q * jnp.exp(g - g_mid)).astype(BF16)
            kj = (k * jnp.exp(g_mid - g)).astype(BF16)
            a_diag = _dot_nt(qi, kj)
            qo = jnp.where(second, q * jnp.exp(jnp.minimum(g - g_c0, 0.0)), 0.0).astype(BF16)
            ko = jnp.where(second, 0.0, k * jnp.exp(jnp.minimum(g_c0 - g, 0.0))).astype(BF16)
            a = (jnp.where(same_chunk_causal, a_diag, 0.0) + _dot_nt(qo, ko)).astype(BF16)

            st = state_ref[bb, h]
            o_inter = _dot((q * jnp.exp(g)).astype(BF16), st.astype(BF16))
            gt = g.T
            g_last_col = gt[:, n - 1:n]
            k_st_t = (k.T * jnp.exp(g_last_col - gt)).astype(BF16)
            state_ref[bb, h] = st * jnp.exp(g_last_col) + _dot(k_st_t, v)
            pend.append((a, v, o_inter, vs))

        for a, v, o_inter, vs in pend:
            o = _dot(a, v) + o_inter
            og = og_ref[bb, r0:r0 + n, vs].astype(F32)
            o_ref[bb, r0:r0 + n, vs] = (_rms(o, nw) * og).astype(BF16)


def _gla(gq, gk, gv, la, og, nw, B, S):
    T = B * S
    tb = TOK_BLK
    bps = GLA_BPS
    blkmap = lambda b, i: (b, i, 0)
    r3 = lambda a: a.reshape(B, S, a.shape[-1])
    out = pl.pallas_call(
        _gla_body,
        grid=(B // bps, S // tb),
        in_specs=[
            pl.BlockSpec((bps, tb, GLA_QK), blkmap),
            pl.BlockSpec((bps, tb, GLA_QK), blkmap),
            pl.BlockSpec((bps, tb, GLA_V), blkmap),
            pl.BlockSpec((bps, tb, GLA_QK), blkmap),
            pl.BlockSpec((bps, tb, GLA_V), blkmap),
            pl.BlockSpec((1, GLA_DV), lambda b, i: (0, 0)),
        ],
        out_specs=pl.BlockSpec((bps, tb, GLA_V), blkmap),
        out_shape=jax.ShapeDtypeStruct((B, S, GLA_V), BF16),
        scratch_shapes=[pltpu.VMEM((bps, GLA_HEADS, GLA_DK, GLA_DV), F32)],
        compiler_params=pltpu.CompilerParams(
            dimension_semantics=("parallel", "arbitrary"), vmem_limit_bytes=VMEM_LIMIT),
        name="gla",
    )(r3(gq), r3(gk), r3(gv), r3(la), r3(og), nw)
    return out.reshape(T, GLA_V)


def _attn_body(qt_ref, qtn_ref, k_ref, vt_ref, lq1_ref, lk1_ref, lq2_ref, lk2_ref, nw_ref, o_ref,
               acc_ref, s_ref, *, lam_init):
    i = pl.program_id(2)
    blk = TOK_BLK
    chains = [(hh, sub) for hh in range(ATT_HPS) for sub in range(2)]
    nc = len(chains)

    def split_subheads(ref):
        out = []
        for hh in range(ATT_HPS):
            qt = ref[hh]
            frow = lax.broadcasted_iota(jnp.int32, qt.shape, 0)
            out.append((jnp.where(frow < DIFF_DH, qt, jnp.zeros_like(qt)),
                        jnp.where(frow >= DIFF_DH, qt, jnp.zeros_like(qt))))
        return out

    qs = split_subheads(qt_ref)
    qs_next = split_subheads(qtn_ref)

    kpt = ATT_KPT

    def issue_scores(jp, c, q=None):
        hh, sub = chains[c]
        kb = k_ref[pl.ds(pl.multiple_of(jp * (kpt * blk), kpt * blk), kpt * blk),
                   hh * 2 * DIFF_DH:(hh + 1) * 2 * DIFF_DH]
        s_ref[c] = _dot(kb, (qs if q is None else q)[hh][sub])

    def softmax_values(jb, nb, c, m, keep):
        hh, sub = chains[c]
        s = s_ref[c, :nb * blk]
        if keep is not None:
            s = jnp.where(keep, s, NEG_BIG)
        m_new = jnp.maximum(m, jnp.max(s, axis=0, keepdims=True))
        p = jnp.exp2(s - m_new).astype(BF16)
        vt = jnp.concatenate([vt_ref[jb + r, hh] for r in range(nb)], axis=1)
        vx = jnp.concatenate([vt, jnp.ones((ATT_ONES_ROWS, nb * blk), BF16)], axis=0)
        acc_ref[c] = acc_ref[c] * jnp.exp2(m - m_new) + _dot(vx, p)
        return m_new

    acc_ref[...] = jnp.zeros_like(acc_ref)

    @pl.when(i == 0)
    def _():
        for c in range(nc):
            issue_scores(0, c)

    crow = lax.broadcasted_iota(jnp.int32, (nc, blk), 0)

    def body(t, m_all):
        for c in range(nc):
            m_new = softmax_values(kpt * t, kpt, c, m_all[c:c + 1], None)
            m_all = jnp.where(crow == c, m_new, m_all)
            issue_scores(t + 1, c)
        return m_all

    npair = i // kpt
    m_all = lax.fori_loop(0, npair, body, jnp.full((nc, blk), NEG_BIG, F32))

    lam = (jnp.exp(jnp.sum(lq1_ref[...] * lk1_ref[...], axis=1, keepdims=True))
           - jnp.exp(jnp.sum(lq2_ref[...] * lk2_ref[...], axis=1, keepdims=True)) + lam_init)

    def finish(nb, keep):
        for hh in range(ATT_HPS):
            for c in (2 * hh, 2 * hh + 1):
                softmax_values(kpt * npair, nb, c, m_all[c:c + 1], keep)
                issue_scores(0, c, qs_next)
            a1, a2 = acc_ref[2 * hh], acc_ref[2 * hh + 1]
            o = (a1[:DIFF_DV] / a1[DIFF_DV:DIFF_DV + 1]
                 - lam * (a2[:DIFF_DV] / a2[DIFF_DV:DIFF_DV + 1]))
            msq = jnp.mean(o * o, axis=0, keepdims=True)
            o = o * lax.rsqrt(msq + EPS) * nw_ref[...] * (1.0 - lam_init)
            o_ref[:, hh * DIFF_DV:(hh + 1) * DIFF_DV] = o.T.astype(BF16)

    kr = lax.broadcasted_iota(jnp.int32, (kpt * blk, blk), 0)
    qc = lax.broadcasted_iota(jnp.int32, (kpt * blk, blk), 1)

    @pl.when(i % kpt == 0)
    def _():
        finish(1, (kr <= qc)[:blk])

    @pl.when(i % kpt == 1)
    def _():
        finish(2, (kr < blk) | (kr - blk <= qc))


def _attn(dqt, dk, dvt, lq1, lk1, lq2, lk2, nw_col, B, S, lam_init):
    T = B * S
    blk = TOK_BLK
    nblk = S // blk
    H, hps = DIFF_HEADS, ATT_HPS
    qt5 = dqt.reshape(B, nblk, H, 2 * DIFF_DH, blk)
    vt5 = dvt.reshape(B, nblk, H, DIFF_DV, blk)
    k3 = dk.reshape(B, S, DIFF_QK)
    vec = pl.BlockSpec((1, DIFF_DH), lambda b, h, i: (0, 0))
    out = pl.pallas_call(
        functools.partial(_attn_body, lam_init=lam_init),
        grid=(B, H // hps, nblk),
        in_specs=[
            pl.BlockSpec((None, None, hps, 2 * DIFF_DH, blk), lambda b, h, i: (b, i, h, 0, 0)),
            pl.BlockSpec((None, None, hps, 2 * DIFF_DH, blk),
                         lambda b, h, i: (b, jnp.minimum(i + 1, nblk - 1), h, 0, 0)),
            pl.BlockSpec((None, S, hps * 2 * DIFF_DH), lambda b, h, i: (b, 0, h)),
            pl.BlockSpec((None, nblk, hps, DIFF_DV, blk), lambda b, h, i: (b, 0, h, 0, 0)),
            vec, vec, vec, vec,
            pl.BlockSpec((DIFF_DV, 1), lambda b, h, i: (0, 0)),
        ],
        out_specs=pl.BlockSpec((None, blk, hps * DIFF_DV), lambda b, h, i: (b, i, h)),
        out_shape=jax.ShapeDtypeStruct((B, S, DIFF_V), BF16),
        scratch_shapes=[pltpu.VMEM((hps * 2, DIFF_DV + ATT_ONES_ROWS, blk), F32),
                        pltpu.VMEM((hps * 2, ATT_KPT * blk, blk), F32)],
        compiler_params=pltpu.CompilerParams(
            dimension_semantics=("arbitrary", "arbitrary", "arbitrary"), vmem_limit_bytes=VMEM_LIMIT),
        name="attn",
    )(qt5, qt5, k3, vt5, lq1, lk1, lq2, lk2, nw_col)
    return out.reshape(T, DIFF_V)


def _merge_body(x_ref, oa_ref, ob_ref, ga_ref, gb_ref, wa_ref, wb_ref, wo_ref, pw_ref, o_ref):
    ya = _dot(oa_ref[...], wa_ref[...])
    yb = _dot(ob_ref[...], wb_ref[...])
    mix = ga_ref[...].astype(F32) * ya + gb_ref[...].astype(F32) * yb
    mixed = _dot(mix.astype(BF16), wo_ref[...])
    o_ref[...] = x_ref[...] + _rms(mixed, pw_ref[...])


def _merge(x2, oa, ob, gates, wa, wb, wo, pw):
    T = x2.shape[0]
    tm = MERGE_TM
    row = lambda i: (i, 0)
    return pl.pallas_call(
        _merge_body,
        grid=(T // tm,),
        in_specs=[
            pl.BlockSpec((tm, D_MODEL), row),
            pl.BlockSpec((tm, GLA_V), row),
            pl.BlockSpec((tm, DIFF_V), row),
            pl.BlockSpec((tm, D_MODEL), lambda i: (i, 0)),
            pl.BlockSpec((tm, D_MODEL), lambda i: (i, 1)),
            _const_spec((GLA_V, D_MODEL)),
            _const_spec((DIFF_V, D_MODEL)),
            _const_spec((D_MODEL, D_MODEL)),
            _const_spec((1, D_MODEL)),
        ],
        out_specs=pl.BlockSpec((tm, D_MODEL), row),
        out_shape=jax.ShapeDtypeStruct((T, D_MODEL), F32),
        compiler_params=pltpu.CompilerParams(
            dimension_semantics=("parallel",), vmem_limit_bytes=VMEM_LIMIT),
        name="merge",
    )(x2, oa, ob, gates, gates, wa, wb, wo, pw)


def _mlp_body(x_ref, wu_ref, wd_ref, prew_ref, postw_ref, o_ref):
    x = x_ref[...]
    h = _rms(x, prew_ref[...]).astype(BF16)
    f = None
    for c in range(D_FF // FF_CHUNK):
        a = jnp.maximum(_dot(h, wu_ref[:, c * FF_CHUNK:(c + 1) * FF_CHUNK]), 0.0)
        part = _dot((a * a).astype(BF16), wd_ref[c * FF_CHUNK:(c + 1) * FF_CHUNK, :])
        f = part if f is None else f + part
    o_ref[...] = x + _rms(f, postw_ref[...])


def _mlp(x1, wu, wd, prew, postw):
    T = x1.shape[0]
    tm = MLP_TM
    row = lambda i: (i, 0)
    return pl.pallas_call(
        _mlp_body,
        grid=(T // tm,),
        in_specs=[
            pl.BlockSpec((tm, D_MODEL), row),
            _const_spec((D_MODEL, D_FF)),
            _const_spec((D_FF, D_MODEL)),
            _const_spec((1, D_MODEL)),
            _const_spec((1, D_MODEL)),
        ],
        out_specs=pl.BlockSpec((tm, D_MODEL), row),
        out_shape=jax.ShapeDtypeStruct((T, D_MODEL), F32),
        compiler_params=pltpu.CompilerParams(
            dimension_semantics=("parallel",), vmem_limit_bytes=VMEM_LIMIT),
        name="mlp",
    )(x1, wu, wd, prew, postw)


def _merge_mlp_body(x_ref, oa_ref, ob_ref, ga_ref, gb_ref, wa_ref, wb_ref, wo_ref, pw_ref,
                    wu_ref, wd_ref, prew_ref, postw_ref, o_ref):
    ya = _dot(oa_ref[...], wa_ref[...])
    yb = _dot(ob_ref[...], wb_ref[...])
    mix = ga_ref[...].astype(F32) * ya + gb_ref[...].astype(F32) * yb
    mixed = _dot(mix.astype(BF16), wo_ref[...])
    x1 = x_ref[...] + _rms(mixed, pw_ref[...])
    h = _rms(x1, prew_ref[...]).astype(BF16)
    f = None
    for c in range(D_FF // FF_CHUNK):
        a = jnp.maximum(_dot(h, wu_ref[:, c * FF_CHUNK:(c + 1) * FF_CHUNK]), 0.0)
        part = _dot((a * a).astype(BF16), wd_ref[c * FF_CHUNK:(c + 1) * FF_CHUNK, :])
        f = part if f is None else f + part
    o_ref[...] = x1 + _rms(f, postw_ref[...])


def _merge_mlp(x2, oa, ob, gates, wa, wb, wo, pw, wu, wd, prew, postw):
    T = x2.shape[0]
    tm = FUSED_TM
    row = lambda i: (i, 0)
    return pl.pallas_call(
        _merge_mlp_body,
        grid=(T // tm,),
        in_specs=[
            pl.BlockSpec((tm, D_MODEL), row),
            pl.BlockSpec((tm, GLA_V), row),
            pl.BlockSpec((tm, DIFF_V), row),
            pl.BlockSpec((tm, D_MODEL), lambda i: (i, 0)),
            pl.BlockSpec((tm, D_MODEL), lambda i: (i, 1)),
            _const_spec((GLA_V, D_MODEL)),
            _const_spec((DIFF_V, D_MODEL)),
            _const_spec((D_MODEL, D_MODEL)),
            _const_spec((1, D_MODEL)),
            _const_spec((D_MODEL, D_FF)),
            _const_spec((D_FF, D_MODEL)),
            _const_spec((1, D_MODEL)),
            _const_spec((1, D_MODEL)),
        ],
        out_specs=pl.BlockSpec((tm, D_MODEL), row),
        out_shape=jax.ShapeDtypeStruct((T, D_MODEL), F32),
        compiler_params=pltpu.CompilerParams(
            dimension_semantics=("parallel",), vmem_limit_bytes=VMEM_LIMIT),
        name="merge_mlp",
    )(x2, oa, ob, gates, gates, wa, wb, wo, pw, wu, wd, prew, postw)


def _rope_tables(positions):
    T = positions.size
    inv_freq = ROPE_THETA ** (-jnp.arange(0, ROT_DIM, 2, dtype=F32) / ROT_DIM)
    ang = positions.astype(F32).reshape(T, 1) * inv_freq
    cos, sin = jnp.cos(ang), jnp.sin(ang)
    cs = jnp.concatenate([cos, sin], axis=1)
    ct = cos.reshape(T // TOK_BLK, TOK_BLK, ROT_HALF).transpose(0, 2, 1)
    st = sin.reshape(T // TOK_BLK, TOK_BLK, ROT_HALF).transpose(0, 2, 1)
    return cs, ct, st


def _rope_expand_matrix():
    import numpy as np
    e = np.zeros((2 * ROT_HALF, 2 * LANES), np.float32)
    for base in range(0, LANES, DIFF_DH):
        for j in range(ROT_HALF):
            e[j, base + j] = 1.0
            e[j, base + ROT_HALF + j] = 1.0
            e[ROT_HALF + j, LANES + base + j] = -1.0
            e[ROT_HALF + j, LANES + base + ROT_HALF + j] = 1.0
    return jnp.asarray(e, BF16)


def kernel(x, positions, w_in, b_gate, w_gk_up, b_gk, gla_norm_w, lambda_q1, lambda_k1, lambda_q2,
           lambda_k2, diff_norm_w, w_branch_a, w_branch_b, w_out, pre_mix_w, post_mix_w, pre_mlp_w,
           post_mlp_w, w_up, w_down):
    B, S, D = x.shape
    T = B * S
    depth = w_in.shape[0]
    cs, ct, st = _rope_tables(positions)
    rope_e = _rope_expand_matrix()
    offs = [0]
    for n in IN_SIZES:
        offs.append(offs[-1] + n)
    (o_gq, o_gk, o_gv, o_og, o_low, o_dq, o_dk, o_dv, o_gates, o_end) = offs

    x2 = x.reshape(T, D)
    for l in range(depth):
        lam_init = 0.8 - 0.6 * math.exp(-0.3 * l)
        w = w_in[l]
        wnat = jnp.concatenate(
            [w[:, o_gq:o_gk], w[:, o_gk:o_gv], w[:, o_gv:o_og], w[:, o_og:o_low],
             w[:, o_dk:o_dv], w[:, o_gates:o_end]], axis=1).astype(BF16)
        wlow = jnp.pad(w[:, o_low:o_dq], ((0, 0), (0, MXU_COLS - GLA_RANK))).astype(BF16)
        wgk = jnp.pad(w_gk_up[l], ((0, MXU_COLS - GLA_RANK), (0, 0))).astype(BF16)
        wt = jnp.concatenate([w[:, o_dq:o_dk], w[:, o_dv:o_gates]], axis=1).T.astype(BF16)

        gq, gk, gv, og, la, dk, gates, dqt, dvt = _inproj(
            x2, pre_mix_w[l].reshape(1, D), wnat, wlow, wgk, b_gk[l].reshape(1, -1),
            b_gate[l].reshape(1, -1), wt, cs, rope_e, ct, st)

        o_a = _gla(gq, gk, gv, la, og, gla_norm_w[l].reshape(1, -1), B, S)
        o_b = _attn(dqt, dk, dvt,
                    lambda_q1[l].reshape(1, -1), lambda_k1[l].reshape(1, -1),
                    lambda_q2[l].reshape(1, -1), lambda_k2[l].reshape(1, -1),
                    diff_norm_w[l].reshape(-1, 1), B, S, lam_init)

        x2 = _merge_mlp(x2, o_a, o_b, gates, w_branch_a[l].astype(BF16), w_branch_b[l].astype(BF16),
                        w_out[l].astype(BF16), post_mix_w[l].reshape(1, D),
                        w_up[l].astype(BF16), w_down[l].astype(BF16),
                        pre_mlp_w[l].reshape(1, D), post_mlp_w[l].reshape(1, D))
    return x2.reshape(B, S, D)
```
